```python
import math
import jax
import jax.numpy as jnp
from jax import lax
import numpy as np

D_MODEL = 2048
BATCH = 2
SEQ = 16384
DEPTH = 4

F32 = jnp.float32
GRID_W = 64
CTX_LEN = 256
HEAD_DIM = 128
MIX_HEADS = D_MODEL // HEAD_DIM
MIX_WIDTH = MIX_HEADS * HEAD_DIM
DIFF_HEADS = MIX_HEADS // 4
DIFF_QK_DIM = HEAD_DIM // 2
DIFF_V_DIM = HEAD_DIM
DIFF_BLOCK = 128
GDN_HEADS = (MIX_HEADS - DIFF_HEADS) // 2
GDN_DK = HEAD_DIM
GDN_DV = HEAD_DIM
GDN_CONV = 5
GDN_CHUNK = 64
SWA_HEADS = MIX_HEADS - DIFF_HEADS - GDN_HEADS
SWA_KV_HEADS = SWA_HEADS // 3
SWA_GROUP = SWA_HEADS // SWA_KV_HEADS
SWA_WINDOW = 128
SWA_BLOCK = 128
D_FF = 5632
FFN_CONV = 3
ROPE_BASE = 10000.0
EPS = 1e-6

GDN_QK_W = GDN_HEADS * GDN_DK
GDN_QKV_W = 2 * GDN_QK_W + GDN_HEADS * GDN_DV
GDN_Z_W = GDN_HEADS * GDN_DV
SWA_Q_W = SWA_HEADS * HEAD_DIM
SWA_KV_W = SWA_KV_HEADS * HEAD_DIM
DIFF_QK_W = DIFF_HEADS * 2 * DIFF_QK_DIM
DIFF_V_W = DIFF_HEADS * DIFF_V_DIM
IN_SPLITS = (GDN_QKV_W, GDN_Z_W, 2 * GDN_HEADS, 2 * GDN_HEADS,
             SWA_Q_W, SWA_KV_W, SWA_KV_W, DIFF_QK_W, DIFF_QK_W, DIFF_V_W)
IN_WIDTH = sum(IN_SPLITS)
IN_OFFSETS = tuple(int(v) for v in np.cumsum(IN_SPLITS)[:-1])

kernel_name = 'hybrid_gdn_swa_diff_dit_block'


def rms_norm(x, g):
    xf = x.astype(F32)
    y = xf * lax.rsqrt(jnp.mean(xf * xf, axis=-1, keepdims=True) + EPS)
    return (y * g.astype(F32)).astype(x.dtype)


def l2_normalize(x):
    return x * lax.rsqrt(jnp.sum(x * x, axis=-1, keepdims=True) + EPS)


def modulate(h, shift, scale):
    return h * (1.0 + scale) + shift


def dwconv_centred(x, w):
    k = w.shape[0]
    return lax.conv_general_dilated(
        x, w[:, None, :].astype(x.dtype), window_strides=(1,),
        padding=[(k // 2, k // 2)], dimension_numbers=('NWC', 'WIO', 'NWC'),
        feature_group_count=x.shape[-1])


def axial_rope_tables(n_tokens, dim):
    n_rows = n_tokens // GRID_W
    quarter = dim // 4
    inv_freq = ROPE_BASE ** (-jnp.arange(quarter, dtype=F32) / quarter)
    rows = jnp.broadcast_to(jnp.arange(n_rows, dtype=F32)[:, None], (n_rows, GRID_W)).reshape(-1)
    cols = jnp.broadcast_to(jnp.arange(GRID_W, dtype=F32)[None, :], (n_rows, GRID_W)).reshape(-1)
    ang_r = rows[:, None] * inv_freq
    ang_c = cols[:, None] * inv_freq
    ang = jnp.concatenate([ang_r, ang_r, ang_c, ang_c], axis=-1)
    return jnp.cos(ang), jnp.sin(ang)


def apply_axial_rope(x, cos, sin):
    shape = (x.shape[1],) + (1,) * (x.ndim - 3) + (x.shape[-1],)
    cos = cos.reshape(shape).astype(x.dtype)
    sin = sin.reshape(shape).astype(x.dtype)
    x1, x2, x3, x4 = jnp.split(x, 4, axis=-1)
    rot = jnp.concatenate([-x2, x1, -x4, x3], axis=-1)
    return x * cos + rot * sin


def gdn_features(qkv, beta_raw, alpha_raw, conv_w, a_log, dt_bias):
    b, l, _ = qkv.shape
    qkv = jax.nn.silu(dwconv_centred(qkv, conv_w)).astype(F32)
    q, k, v = jnp.split(qkv, [GDN_QK_W, 2 * GDN_QK_W], axis=-1)
    q = l2_normalize(q.reshape(b, l, GDN_HEADS, GDN_DK)) * GDN_DK ** -0.5
    k = l2_normalize(k.reshape(b, l, GDN_HEADS, GDN_DK))
    v = v.reshape(b, l, GDN_HEADS, GDN_DV)
    beta = jax.nn.sigmoid(beta_raw.astype(F32)).reshape(b, l, 2, GDN_HEADS)
    g = -jnp.exp(a_log.astype(F32)) * jax.nn.softplus(
        alpha_raw.astype(F32).reshape(b, l, 2, GDN_HEADS) + dt_bias.astype(F32))
    return q, k, v, beta, g


def gdn_dir_inputs(feat, d, reverse):
    q, k, v, beta, g = feat
    t = (q, k, v, beta[:, :, d], g[:, :, d])
    if reverse:
        return tuple(jnp.flip(a, axis=1) for a in t)
    return t


def gdn_prepare(q, k, v, beta, g):
    b, l, h, _ = k.shape
    dv = v.shape[-1]
    n = l // GDN_CHUNK

    def chunks(t):
        t = t.reshape((b, n, GDN_CHUNK, h) + t.shape[3:])
        return jnp.moveaxis(t, (1, 3), (0, 2))

    q, k, v, beta, g = (chunks(t) for t in (q, k, v, beta, g))
    cum = jnp.cumsum(g, axis=-1)
    idx = jnp.arange(GDN_CHUNK)
    incl = idx[:, None] >= idx[None, :]
    strict = idx[:, None] > idx[None, :]
    decay = jnp.exp(jnp.where(incl, cum[..., :, None] - cum[..., None, :], -jnp.inf))
    a = jnp.where(strict, beta[..., :, None] * jnp.einsum('nbhid,nbhjd->nbhij', k, k) * decay, 0.0)
    rhs = jnp.concatenate([beta[..., None] * v, (beta * jnp.exp(cum))[..., None] * k], axis=-1)
    sol = lax.linalg.triangular_solve(a, rhs, left_side=True, lower=True, unit_diagonal=True)
    u, w = sol[..., :dv], sol[..., dv:]
    qk = jnp.einsum('nbhid,nbhjd->nbhij', q, k) * decay
    q_dec = q * jnp.exp(cum)[..., None]
    k_dec = k * jnp.exp(cum[..., -1:] - cum)[..., None]
    return u, w, qk, q_dec, k_dec, jnp.exp(cum[..., -1])


def gdn_scan(state0, prep):
    def step(s, inp):
        u, w, qk, q_dec, k_dec, dec = inp
        v_new = u - jnp.einsum('bhck,bhkv->bhcv', w, s)
        o = jnp.einsum('bhck,bhkv->bhcv', q_dec, s) + jnp.einsum('bhij,bhjv->bhiv', qk, v_new)
        s = s * dec[..., None, None] + jnp.einsum('bhck,bhcv->bhkv', k_dec, v_new)
        return s, o
    s_final, o = lax.scan(step, state0, prep)
    n, b, h, c, dv = o.shape
    return s_final, jnp.moveaxis(o, (0, 2), (1, 3)).reshape(b, n * c, h, dv)


def gdn_two_segment(ctx_in, lat_in):
    b, _, h, dk = ctx_in[1].shape
    dv = ctx_in[2].shape[-1]
    s0 = jnp.zeros((b, h, dk, dv), F32)
    s_ctx, o_ctx = gdn_scan(s0, gdn_prepare(*ctx_in))
    _, o_lat = gdn_scan(s_ctx, gdn_prepare(*lat_in))
    return o_ctx, o_lat


def gdn_output(o, z, norm_w):
    b, l, h, dv = o.shape
    gate = jax.nn.silu(z.astype(F32)).reshape(b, l, h, dv)
    return (rms_norm(o, norm_w) * gate).reshape(b, l, h * dv).astype(z.dtype)


def window_attention_latent(q, k, v, k_ctx, v_ctx, sink):
    b, s, hkv, grp, d = q.shape
    nb = s // SWA_BLOCK
    nw = 3 * SWA_BLOCK
    n_ctx = k_ctx.shape[1]
    qb = q.reshape(b, nb, SWA_BLOCK, hkv, grp, d)

    def band(t):
        tp = jnp.pad(t, ((0, 0), (SWA_BLOCK, SWA_BLOCK), (0, 0), (0, 0)))
        tp = tp.reshape(b, nb + 2, SWA_BLOCK, hkv, d)
        return jnp.concatenate([tp[:, :-2], tp[:, 1:-1], tp[:, 2:]], axis=2)

    kw, vw = band(k), band(v)
    scale = d ** -0.5
    s_win = jnp.einsum('bnqhgd,bnkhd->bhgnqk', qb, kw).astype(F32) * scale
    qpos = jnp.arange(nb)[:, None, None] * SWA_BLOCK + jnp.arange(SWA_BLOCK)[None, :, None]
    kpos = (jnp.arange(nb)[:, None, None] - 1) * SWA_BLOCK + jnp.arange(nw)[None, None, :]
    mask = (jnp.abs(kpos - qpos) <= SWA_WINDOW) & (kpos >= 0) & (kpos < s)
    s_win = jnp.where(mask, s_win, -jnp.inf)
    s_ctx = jnp.einsum('bnqhgd,bchd->bhgnqc', qb, k_ctx).astype(F32) * scale
    sink_col = jnp.broadcast_to(sink.astype(F32).reshape(1, hkv, grp, 1, 1, 1), s_win.shape[:-1] + (1,))
    p = jax.nn.softmax(jnp.concatenate([s_win, s_ctx, sink_col], axis=-1), axis=-1).astype(v.dtype)
    o = (jnp.einsum('bhgnqk,bnkhd->bnqhgd', p[..., :nw], vw)
         + jnp.einsum('bhgnqc,bchd->bnqhgd', p[..., nw:nw + n_ctx], v_ctx))
    return o.reshape(b, s, hkv * grp * d)


def window_attention_context(q, k, v, sink):
    hkv, grp = sink.shape
    s = jnp.einsum('bchgd,bkhd->bhgck', q, k).astype(F32) * q.shape[-1] ** -0.5
    sink_col = jnp.broadcast_to(sink.astype(F32)[None, :, :, None, None], s.shape[:-1] + (1,))
    p = jax.nn.softmax(jnp.concatenate([s, sink_col], axis=-1), axis=-1)[..., :-1].astype(v.dtype)
    o = jnp.einsum('bhgck,bkhd->bchgd', p, v)
    return o.reshape(o.shape[0], o.shape[1], -1)


def diff_lambda_value(lam, layer):
    lam_init = 0.8 - 0.6 * math.exp(-0.3 * layer)
    lf = lam.astype(F32)
    lam_full = jnp.exp(jnp.sum(lf[0] * lf[1])) - jnp.exp(jnp.sum(lf[2] * lf[3])) + lam_init
    return lam_full, lam_init


def diff_attend(q, k, v, lam_full):
    s = jnp.einsum('bqhmd,bkhmd->bhmqk', q, k).astype(F32) * DIFF_QK_DIM ** -0.5
    p = jax.nn.softmax(s, axis=-1)
    w = (p[:, :, 0] - lam_full * p[:, :, 1]).astype(v.dtype)
    return jnp.einsum('bhqk,bkhd->bqhd', w, v)


def diff_attention_latent(q, k_all, v_all, lam_full):
    b, s, h, m, dq = q.shape
    nb = s // DIFF_BLOCK
    qb = jnp.moveaxis(q.reshape(b, nb, DIFF_BLOCK, h, m, dq), 1, 0)
    o = lax.map(lambda qi: diff_attend(qi, k_all, v_all, lam_full), qb)
    return jnp.moveaxis(o, 0, 1).reshape(b, s, h, v_all.shape[-1])


def mixing(h_lat, h_ctx, rope, w_in, gdn_conv, gdn_a_log, gdn_dt_bias, gdn_norm,
           swa_sink, diff_lam, diff_norm, w_out, layer, with_ctx):
    b, s, _ = h_lat.shape
    n_ctx = h_ctx.shape[1]
    cos_b, sin_b, cos_c, sin_c = rope
    (lat_qkv, lat_z, lat_beta, lat_alpha, lat_sq, lat_sk, lat_sv,
     lat_dq, lat_dk, lat_dv) = jnp.split(h_lat @ w_in, IN_OFFSETS, axis=-1)
    (cx_qkv, cx_z, cx_beta, cx_alpha, cx_sq, cx_sk, cx_sv,
     cx_dq, cx_dk, cx_dv) = jnp.split(h_ctx @ w_in, IN_OFFSETS, axis=-1)

    f_lat = gdn_features(lat_qkv, lat_beta, lat_alpha, gdn_conv, gdn_a_log, gdn_dt_bias)
    f_ctx = gdn_features(cx_qkv, cx_beta, cx_alpha, gdn_conv, gdn_a_log, gdn_dt_bias)
    oc_f, ol_f = gdn_two_segment(gdn_dir_inputs(f_ctx, 0, False), gdn_dir_inputs(f_lat, 0, False))
    oc_b, ol_b = gdn_two_segment(gdn_dir_inputs(f_ctx, 1, True), gdn_dir_inputs(f_lat, 1, True))
    a_lat = gdn_output(ol_f + jnp.flip(ol_b, axis=1), lat_z, gdn_norm)

    sink = swa_sink.reshape(SWA_KV_HEADS, SWA_GROUP)
    sq_lat = apply_axial_rope(lat_sq.reshape(b, s, SWA_KV_HEADS, SWA_GROUP, HEAD_DIM), cos_b, sin_b)
    sk_lat = apply_axial_rope(lat_sk.reshape(b, s, SWA_KV_HEADS, HEAD_DIM), cos_b, sin_b)
    sv_lat = lat_sv.reshape(b, s, SWA_KV_HEADS, HEAD_DIM)
    sk_ctx = cx_sk.reshape(b, n_ctx, SWA_KV_HEADS, HEAD_DIM)
    sv_ctx = cx_sv.reshape(b, n_ctx, SWA_KV_HEADS, HEAD_DIM)
    b_lat = window_attention_latent(sq_lat, sk_lat, sv_lat, sk_ctx, sv_ctx, sink)

    lam_full, lam_init = diff_lambda_value(diff_lam, layer)
    dq_lat = apply_axial_rope(lat_dq.reshape(b, s, DIFF_HEADS, 2, DIFF_QK_DIM), cos_c, sin_c)
    dk_lat = apply_axial_rope(lat_dk.reshape(b, s, DIFF_HEADS, 2, DIFF_QK_DIM), cos_c, sin_c)
    dv_lat = lat_dv.reshape(b, s, DIFF_HEADS, DIFF_V_DIM)
    dk_ctx = cx_dk.reshape(b, n_ctx, DIFF_HEADS, 2, DIFF_QK_DIM)
    dv_ctx = cx_dv.reshape(b, n_ctx, DIFF_HEADS, DIFF_V_DIM)
    c_lat = diff_attention_latent(dq_lat, jnp.concatenate([dk_lat, dk_ctx], axis=1),
                                  jnp.concatenate([dv_lat, dv_ctx], axis=1), lam_full)
    c_lat = (rms_norm(c_lat, diff_norm) * (1.0 - lam_init)).reshape(b, s, DIFF_V_W)

    y_lat = jnp.concatenate([a_lat, b_lat, c_lat], axis=-1) @ w_out
    if not with_ctx:
        return y_lat, None

    a_ctx = gdn_output(oc_f + jnp.flip(oc_b, axis=1), cx_z, gdn_norm)
    b_ctx = window_attention_context(cx_sq.reshape(b, n_ctx, SWA_KV_HEADS, SWA_GROUP, HEAD_DIM),
                                     sk_ctx, sv_ctx, sink)
    c_ctx_o = diff_attend(cx_dq.reshape(b, n_ctx, DIFF_HEADS, 2, DIFF_QK_DIM), dk_ctx, dv_ctx, lam_full)
    c_ctx_o = (rms_norm(c_ctx_o, diff_norm) * (1.0 - lam_init)).reshape(b, n_ctx, DIFF_V_W)
    y_ctx = jnp.concatenate([a_ctx, b_ctx, c_ctx_o], axis=-1) @ w_out
    return y_lat, y_ctx


def conv_ffn(h, w_gate, w_up, conv_w, w_down):
    return (jax.nn.silu(dwconv_centred(h @ w_gate, conv_w)) * (h @ w_up)) @ w_down


def setup_inputs(seed: int = 0) -> dict:
    key = jax.random.key(seed)
    ks = jax.random.split(key, 24)

    def nrm(k, shape, scale):
        return jax.random.normal(k, shape, F32) * scale

    def gain(k, shape):
        return 1.0 + 0.05 * jax.random.normal(k, shape, F32)

    dt = jnp.exp(jax.random.uniform(ks[13], (DEPTH, 2, GDN_HEADS), F32, math.log(1e-3), math.log(1e-1)))
    return {
        'x': nrm(ks[0], (BATCH, SEQ, D_MODEL), 1.0),
        'c': nrm(ks[1], (BATCH, D_MODEL), 1.0),
        'ctx': nrm(ks[2], (BATCH, CTX_LEN, D_MODEL), 1.0),
        'c_ctx': nrm(ks[3], (D_MODEL,), 1.0),
        'w_mod': nrm(ks[4], (DEPTH, D_MODEL, 6 * D_MODEL), 0.5 * D_MODEL ** -0.5),
        'b_mod': nrm(ks[5], (DEPTH, 6 * D_MODEL), 0.01),
        'norm_pre_mix': gain(ks[6], (DEPTH, D_MODEL)),
        'norm_post_mix': gain(ks[7], (DEPTH, D_MODEL)),
        'norm_pre_ffn': gain(ks[8], (DEPTH, D_MODEL)),
        'norm_post_ffn': gain(ks[9], (DEPTH, D_MODEL)),
        'w_in': nrm(ks[10], (DEPTH, D_MODEL, IN_WIDTH), D_MODEL ** -0.5),
        'gdn_conv': nrm(ks[11], (DEPTH, GDN_CONV, GDN_QKV_W), GDN_CONV ** -0.5),
        'gdn_a_log': jnp.log(jax.random.uniform(ks[12], (DEPTH, 2, GDN_HEADS), F32, 1.0, 16.0)),
        'gdn_dt_bias': dt + jnp.log(-jnp.expm1(-dt)),
        'gdn_norm': gain(ks[14], (DEPTH, GDN_DV)),
        'swa_sink': nrm(ks[15], (DEPTH, SWA_HEADS), 0.5),
        'diff_lambda': nrm(ks[16], (DEPTH, 4, DIFF_QK_DIM), 0.1),
        'diff_norm': gain(ks[17], (DEPTH, DIFF_V_DIM)),
        'w_out': nrm(ks[18], (DEPTH, MIX_WIDTH, D_MODEL), MIX_WIDTH ** -0.5),
        'ffn_w_gate': nrm(ks[19], (DEPTH, D_MODEL, D_FF), D_MODEL ** -0.5),
        'ffn_w_up': nrm(ks[20], (DEPTH, D_MODEL, D_FF), D_MODEL ** -0.5),
        'ffn_conv': nrm(ks[21], (DEPTH, FFN_CONV, D_FF), FFN_CONV ** -0.5),
        'ffn_w_down': nrm(ks[22], (DEPTH, D_FF, D_MODEL), D_FF ** -0.5),
    }


def reference(x, c, ctx, c_ctx, w_mod, b_mod, norm_pre_mix, norm_post_mix, norm_pre_ffn,
              norm_post_ffn, w_in, gdn_conv, gdn_a_log, gdn_dt_bias, gdn_norm, swa_sink,
              diff_lambda, diff_norm, w_out, ffn_w_gate, ffn_w_up, ffn_conv, ffn_w_down):
    n_lat = x.shape[1]
    cos_b, sin_b = axial_rope_tables(n_lat, HEAD_DIM)
    cos_c, sin_c = axial_rope_tables(n_lat, DIFF_QK_DIM)
    rope = (cos_b, sin_b, cos_c, sin_c)
    for layer in range(DEPTH):
        with_ctx = layer < DEPTH - 1
        mod_w, mod_b = w_mod[layer], b_mod[layer]
        m_lat = jnp.split((jax.nn.silu(c) @ mod_w + mod_b)[:, None, :], 6, axis=-1)
        m_ctx = jnp.split((jax.nn.silu(c_ctx) @ mod_w + mod_b)[None, None, :], 6, axis=-1)
        h_lat = modulate(rms_norm(x, norm_pre_mix[layer]), m_lat[0], m_lat[1])
        h_ctx = modulate(rms_norm(ctx, norm_pre_mix[layer]), m_ctx[0], m_ctx[1])
        y_lat, y_ctx = mixing(h_lat, h_ctx, rope, w_in[layer], gdn_conv[layer], gdn_a_log[layer],
                              gdn_dt_bias[layer], gdn_norm[layer], swa_sink[layer],
                              diff_lambda[layer], diff_norm[layer], w_out[layer], layer, with_ctx)
        x = x + m_lat[2] * rms_norm(y_lat, norm_post_mix[layer])
        f_lat = conv_ffn(modulate(rms_norm(x, norm_pre_ffn[layer]), m_lat[3], m_lat[4]),
                         ffn_w_gate[layer], ffn_w_up[layer], ffn_conv[layer], ffn_w_down[layer])
        x = x + m_lat[5] * rms_norm(f_lat, norm_post_ffn[layer])
        if with_ctx:
            ctx = ctx + m_ctx[2] * rms_norm(y_ctx, norm_post_mix[layer])
            f_ctx = conv_ffn(modulate(rms_norm(ctx, norm_pre_ffn[layer]), m_ctx[3], m_ctx[4]),
                             ffn_w_gate[layer], ffn_w_up[layer], ffn_conv[layer], ffn_w_down[layer])
            ctx = ctx + m_ctx[5] * rms_norm(f_ctx, norm_post_ffn[layer])
    return x
```

```python
import functools
import math

import jax
import jax.numpy as jnp
import numpy as np
from jax import lax
from jax.experimental import pallas as pl
from jax.experimental.pallas import tpu as pltpu

F32 = jnp.float32
BF16 = jnp.bfloat16

GRID_W = 64
HEAD_DIM = 128
LANES = 128
EPS = 1e-6
ROPE_BASE = 10000.0

GDN_HEADS = 6
GDN_CONV = 5
GDN_CHUNK = 64
GDN_W = GDN_HEADS * HEAD_DIM
SWA_HEADS = 6
SWA_KV_HEADS = 2
SWA_GROUP = 3
SWA_WINDOW = 128
SWA_BLOCK = 128
DIFF_HEADS = 4
DIFF_QK_DIM = 64
FFN_CONV = 3

VMEM_LIMIT_BYTES = 56 * 1024 * 1024


def _params(*sem):
    return pltpu.CompilerParams(dimension_semantics=sem, vmem_limit_bytes=VMEM_LIMIT_BYTES)


def _silu(x):
    return x * jax.nn.sigmoid(x)


def _mm_kernel(a_ref, b_ref, o_ref):
    o_ref[...] = jnp.dot(a_ref[...], b_ref[...], preferred_element_type=F32).astype(o_ref.dtype)


def matmul(a, b, out_dtype, tm=512, tn=512, name="matmul"):
    m, k = a.shape
    n = b.shape[1]
    tm = min(tm, m)
    tn = min(tn, n)
    assert m % tm == 0 and n % tn == 0, (m, n, tm, tn)
    return pl.pallas_call(
        _mm_kernel,
        grid=(m // tm, n // tn),
        in_specs=[pl.BlockSpec((tm, k), lambda i, j: (i, 0)),
                  pl.BlockSpec((k, tn), lambda i, j: (0, j))],
        out_specs=pl.BlockSpec((tm, tn), lambda i, j: (i, j)),
        out_shape=jax.ShapeDtypeStruct((m, n), out_dtype),
        compiler_params=_params("parallel", "parallel"),
        name=name,
    )(a, b)


def _mod_kernel(c_ref, w_ref, b_ref, o_ref):
    x = _silu(c_ref[...]).astype(BF16)
    o_ref[...] = jnp.dot(x, w_ref[...].astype(BF16), preferred_element_type=F32) + b_ref[...]


def modulation(cc, w, b):
    m, k = cc.shape
    n = w.shape[1]
    tn = 1024
    return pl.pallas_call(
        _mod_kernel,
        grid=(n // tn,),
        in_specs=[pl.BlockSpec((m, k), lambda j: (0, 0)),
                  pl.BlockSpec((k, tn), lambda j: (0, j)),
                  pl.BlockSpec((1, tn), lambda j: (0, j))],
        out_specs=pl.BlockSpec((m, tn), lambda j: (0, j)),
        out_shape=jax.ShapeDtypeStruct((m, n), F32),
        compiler_params=_params("parallel"),
        name="modulation",
    )(cc, w, b.reshape(1, n))


def _rms(x, g):
    return x * lax.rsqrt(jnp.mean(x * x, axis=-1, keepdims=True) + EPS) * g


def _norm_kernel(has_resid, has_norm, *refs):
    refs = list(refs)
    x = refs.pop(0)[0]
    if has_resid:
        y = refs.pop(0)[0]
        gate = refs.pop(0)[0]
        gpost = refs.pop(0)[...]
        x = x + gate * _rms(y, gpost)
    if has_norm:
        gpre = refs.pop(0)[...]
        shift = refs.pop(0)[0]
        scale = refs.pop(0)[0]
    if has_resid:
        refs.pop(0)[0] = x
    if has_norm:
        refs.pop(0)[0] = (_rms(x, gpre) * (1.0 + scale) + shift).astype(BF16)


def norm_step(x, resid=None, norm=None, ts=512):
    b, s, d = x.shape
    ts = min(ts, s)
    assert s % ts == 0
    tok = pl.BlockSpec((1, ts, d), lambda i, j: (i, j, 0))
    vec = pl.BlockSpec((1, 1, d), lambda i, j: (i, 0, 0))
    gsp = pl.BlockSpec((1, d), lambda i, j: (0, 0))
    args, specs, outs, out_specs = [x], [tok], [], []
    if resid is not None:
        y, gate, gpost = resid
        args += [y, gate, gpost.reshape(1, d)]
        specs += [tok, vec, gsp]
        outs.append(jax.ShapeDtypeStruct((b, s, d), F32))
        out_specs.append(tok)
    if norm is not None:
        gpre, shift, scale = norm
        args += [gpre.reshape(1, d), shift, scale]
        specs += [gsp, vec, vec]
        outs.append(jax.ShapeDtypeStruct((b, s, d), BF16))
        out_specs.append(tok)
    res = pl.pallas_call(
        functools.partial(_norm_kernel, resid is not None, norm is not None),
        grid=(b, s // ts),
        in_specs=specs, out_specs=out_specs, out_shape=outs,
        compiler_params=_params("parallel", "parallel"),
        name="norm_step",
    )(*args)
    res = list(res)
    x_new = res.pop(0) if resid is not None else None
    h = res.pop(0) if norm is not None else None
    return x_new, h


def _seq_conv(x, prev8, next8, w, halo):
    t = x.shape[0]
    xe = jnp.concatenate([prev8, x, next8], axis=0)
    acc = None
    for j in range(2 * halo + 1):
        off = 8 + j - halo
        term = xe[off:off + t] * w[j:j + 1]
        acc = term if acc is None else acc + term
    return acc


def _halo_specs(ts, rows, width, col_of):
    per = ts // rows

    def main(b, i, c):
        return (b, i, col_of(c))

    def prev(b, i, c):
        return (b, jnp.maximum(i * per - 1, 0), col_of(c))

    def nxt(nblk):
        def f(b, i, c):
            return (b, jnp.minimum((i + 1) * per, nblk * per - 1), col_of(c))
        return f

    return main, prev, nxt


def _ffn_act_kernel(g_ref, gp_ref, gn_ref, u_ref, w_ref, o_ref):
    i = pl.program_id(1)
    last = pl.num_programs(1) - 1
    g = g_ref[0].astype(F32)
    prev8 = gp_ref[0].astype(F32)[8:16] * (i > 0).astype(F32)
    next8 = gn_ref[0].astype(F32)[0:8] * (i < last).astype(F32)
    y = _seq_conv(g, prev8, next8, w_ref[...], FFN_CONV // 2)
    o_ref[0] = (_silu(y) * u_ref[0].astype(F32)).astype(BF16)


def ffn_act(g, u, conv_w, ts=512, tc=512):
    b, s, f = g.shape
    ts = min(ts, s)
    nblk = s // ts
    main, prev, nxt = _halo_specs(ts, 16, tc, lambda c: c)
    wpad = jnp.zeros((8, f), F32).at[:FFN_CONV].set(conv_w)
    return pl.pallas_call(
        _ffn_act_kernel,
        grid=(b, nblk, f // tc),
        in_specs=[pl.BlockSpec((1, ts, tc), main),
                  pl.BlockSpec((1, 16, tc), prev),
                  pl.BlockSpec((1, 16, tc), nxt(nblk)),
                  pl.BlockSpec((1, ts, tc), main),
                  pl.BlockSpec((8, tc), lambda bb, i, c: (0, c))],
        out_specs=pl.BlockSpec((1, ts, tc), main),
        out_shape=jax.ShapeDtypeStruct((b, s, f), BF16),
        compiler_params=_params("parallel", "parallel", "parallel"),
        name="ffn_act",
    )(g, g, g, u, wpad)


def _rope_tables(n_tokens, dim):
    quarter = dim // 4
    inv_freq = ROPE_BASE ** (-jnp.arange(quarter, dtype=F32) / quarter)
    tok = jnp.arange(n_tokens)
    rows = (tok // GRID_W).astype(F32)
    cols = (tok % GRID_W).astype(F32)
    ang_r = rows[:, None] * inv_freq
    ang_c = cols[:, None] * inv_freq
    ang = jnp.concatenate([ang_r, ang_r, ang_c, ang_c], axis=-1)
    ang = jnp.tile(ang, (1, LANES // dim))
    cos, sin = jnp.cos(ang), jnp.sin(ang)
    first = (jnp.arange(LANES) % (2 * quarter)) < quarter
    sin_a = jnp.where(first, -sin, 0.0)
    sin_b = jnp.where(first, 0.0, sin)
    return jnp.stack([cos, sin_a, sin_b])


def _rope_kernel(x_ref, tb_ref, tc_ref, o_ref):
    j = pl.program_id(2)
    ngroups = x_ref.shape[2] // LANES

    def apply(tab_ref, quarter, scale):
        cos, sin_a, sin_b = tab_ref[0], tab_ref[1], tab_ref[2]
        for gidx in range(ngroups):
            sl = slice(gidx * LANES, (gidx + 1) * LANES)
            x = x_ref[0, :, sl]
            up = pltpu.roll(x, LANES - quarter, axis=1)
            dn = pltpu.roll(x, quarter, axis=1)
            o_ref[0, :, sl] = ((x * cos + up * sin_a + dn * sin_b) * scale).astype(BF16)

    @pl.when(j == 0)
    def _():
        apply(tb_ref, HEAD_DIM // 4, 1.0)

    @pl.when(j == 1)
    def _():
        apply(tc_ref, DIFF_QK_DIM // 4, 1.0)


def rope_qk(x, tab_b, tab_c, ts=512):
    b, s, w = x.shape
    half = w // 2
    ts = min(ts, s)
    return pl.pallas_call(
        _rope_kernel,
        grid=(b, s // ts, 2),
        in_specs=[pl.BlockSpec((1, ts, half), lambda bb, i, j: (bb, i, j)),
                  pl.BlockSpec((3, ts, LANES), lambda bb, i, j: (0, i, 0)),
                  pl.BlockSpec((3, ts, LANES), lambda bb, i, j: (0, i, 0))],
        out_specs=pl.BlockSpec((1, ts, half), lambda bb, i, j: (bb, i, j)),
        out_shape=jax.ShapeDtypeStruct((b, s, w), BF16),
        compiler_params=_params("parallel", "parallel", "parallel"),
        name="rope_qk",
    )(x, tab_b, tab_c)


def _gdn_feat_kernel(x_ref, xp_ref, xn_ref, w_ref, o_ref):
    i = pl.program_id(1)
    c = pl.program_id(2)
    last = pl.num_programs(1) - 1
    prev8 = xp_ref[0] * (i > 0).astype(F32)
    next8 = xn_ref[0] * (i < last).astype(F32)
    y = _silu(_seq_conv(x_ref[0], prev8, next8, w_ref[...], GDN_CONV // 2))
    inv = lax.rsqrt(jnp.sum(y * y, axis=-1, keepdims=True) + EPS)
    kind = c // GDN_HEADS
    inv = jnp.where(kind == 0, inv * (HEAD_DIM ** -0.5), inv)
    inv = jnp.where(kind == 2, jnp.ones_like(inv), inv)
    o_ref[0] = y * inv


def gdn_features(proj, conv_w, ts=512):
    b, l, _ = proj.shape
    ts = min(ts, l)
    nblk = l // ts
    ncol = 3 * GDN_HEADS
    main, prev, nxt = _halo_specs(ts, 8, LANES, lambda c: c)
    wpad = jnp.zeros((8, ncol * LANES), F32).at[:GDN_CONV].set(conv_w)
    return pl.pallas_call(
        _gdn_feat_kernel,
        grid=(b, nblk, ncol),
        in_specs=[pl.BlockSpec((1, ts, LANES), main),
                  pl.BlockSpec((1, 8, LANES), prev),
                  pl.BlockSpec((1, 8, LANES), nxt(nblk)),
                  pl.BlockSpec((8, LANES), lambda bb, i, c: (0, c))],
        out_specs=pl.BlockSpec((1, ts, LANES), main),
        out_shape=jax.ShapeDtypeStruct((b, l, ncol * LANES), F32),
        compiler_params=_params("parallel", "parallel", "parallel"),
        name="gdn_features",
    )(proj, proj, proj, wpad)


def _split3(x):
    hi = x.astype(BF16)
    r = x - hi.astype(F32)
    mid = r.astype(BF16)
    lo = (r - mid.astype(F32)).astype(BF16)
    return hi, mid, lo


def _dot(a, b):
    return jnp.dot(a, b, preferred_element_type=F32)


def _mask_dot(mask_bf16, x):
    hi, mid, lo = _split3(x)
    return _dot(mask_bf16, hi) + _dot(mask_bf16, mid) + _dot(mask_bf16, lo)


def _dot3(a, b):
    a_hi = a.astype(BF16)
    a_lo = (a - a_hi.astype(F32)).astype(BF16)
    b_hi = b.astype(BF16)
    b_lo = (b - b_hi.astype(F32)).astype(BF16)
    return _dot(a_hi, b_hi) + _dot(a_hi, b_lo) + _dot(a_lo, b_hi)


def _unit_tri_inverse(a, same16, same32):
    n = a.shape[0]
    eye = (lax.broadcasted_iota(jnp.int32, (n, n), 0) == lax.broadcasted_iota(jnp.int32, (n, n), 1)).astype(F32)
    d = jnp.where(same16, a, 0.0)
    l1 = jnp.where(jnp.logical_and(same32, jnp.logical_not(same16)), a, 0.0)
    l2 = jnp.where(same32, 0.0, a)
    d2 = _dot3(d, d)
    d3 = _dot3(d2, d)
    d4 = _dot3(d2, d2)
    p = eye - d + d2 - d3
    p = p + _dot3(p, d4)
    d8 = _dot3(d4, d4)
    t = p + _dot3(p, d8)
    t = t - _dot3(t, _dot3(l1, t))
    t = t - _dot3(t, _dot3(l2, t))
    return t


def _gdn_kernel(reverse, q_ref, k_ref, v_ref, gt_ref, cst_ref, s0_ref, o_ref, sfin_ref,
                s_scr, g_scr, cum_scr, tot_scr, xt_scr):
    i = pl.program_id(1)
    nblk = pl.num_programs(1)
    tb = q_ref.shape[1]
    npairs = tb // 128
    d = 1 if reverse else 0

    @pl.when(i == 0)
    def _():
        s_scr[...] = s0_ref[0]

    raw = gt_ref[0]
    lane = lax.broadcasted_iota(jnp.int32, raw.shape, 1)
    neg_a = cst_ref[0:1, :]
    dt_bias = cst_ref[1:2, :]
    gates = jnp.where(lane < 2 * GDN_HEADS, jax.nn.sigmoid(raw), neg_a * jax.nn.softplus(raw + dt_bias))
    g_scr[...] = gates

    r64 = lax.broadcasted_iota(jnp.int32, (GDN_CHUNK, GDN_CHUNK), 0)
    c64 = lax.broadcasted_iota(jnp.int32, (GDN_CHUNK, GDN_CHUNK), 1)
    incl = (r64 <= c64) if reverse else (r64 >= c64)
    strict = (r64 < c64) if reverse else (r64 > c64)
    same16 = (r64 // 16) == (c64 // 16)
    same32 = (r64 // 32) == (c64 // 32)

    r128 = lax.broadcasted_iota(jnp.int32, (128, 128), 0)
    c128 = lax.broadcasted_iota(jnp.int32, (128, 128), 1)
    same_chunk = (r128 // GDN_CHUNK) == (c128 // GDN_CHUNK)
    order = (r128 <= c128) if reverse else (r128 >= c128)
    cum_mask = jnp.logical_and(same_chunk, order).astype(BF16)
    tot_mask = same_chunk.astype(BF16)
    for p in range(npairs):
        rows = slice(p * 128, (p + 1) * 128)
        gp = gates[rows]
        cum = _mask_dot(cum_mask, gp)
        tot = _mask_dot(tot_mask, gp)
        cum_scr[rows, :] = cum
        tot_scr[rows, :] = tot
        lane_p = lax.broadcasted_iota(jnp.int32, gp.shape, 1)
        xt_scr[p] = jnp.where(lane_p < 2 * GDN_HEADS, gp, cum).T

    def pair_body(pp, carry):
        p = (npairs - 1 - pp) if reverse else pp
        row0 = pl.multiple_of(p * 128, 128)
        gp = g_scr[pl.ds(row0, 128), :]
        cump = cum_scr[pl.ds(row0, 128), :]
        totp = tot_scr[pl.ds(row0, 128), :]
        xtp = xt_scr[p]
        ecum = jnp.exp(cump)
        erem = jnp.exp(totp - cump)
        for h in range(GDN_HEADS):
            hs = slice(h * HEAD_DIM, (h + 1) * HEAD_DIM)
            qp = q_ref[0, pl.ds(row0, 128), hs]
            kp = k_ref[0, pl.ds(row0, 128), hs]
            vp = v_ref[0, pl.ds(row0, 128), hs]
            ktp = kp.T
            bi = d * GDN_HEADS + h
            gi = 2 * GDN_HEADS + bi
            s = s_scr[h]
            outs = [None, None]
            for cc in ((1, 0) if reverse else (0, 1)):
                rs = slice(cc * GDN_CHUNK, (cc + 1) * GDN_CHUNK)
                qh, kh, vh = qp[rs], kp[rs], vp[rs]
                kt = ktp[:, rs]
                bcol = gp[rs, bi:bi + 1]
                ccol = cump[rs, gi:gi + 1]
                ecol = ecum[rs, gi:gi + 1]
                crow = xtp[gi:gi + 1, rs]
                tot11 = totp[cc * GDN_CHUNK:cc * GDN_CHUNK + 1, gi:gi + 1]
                decay = jnp.exp(jnp.where(incl, ccol - crow, -jnp.inf))
                kb = kh.astype(BF16)
                ktb = kt.astype(BF16)
                kk = _dot(kb, ktb)
                qk = _dot(qh.astype(BF16), ktb) * decay
                a = jnp.where(strict, bcol * kk * decay, 0.0)
                t = _unit_tri_inverse(a, same16, same32)
                u = _dot3(t, bcol * vh)
                w = _dot3(t, (bcol * ecol) * kh)
                sb = s.astype(BF16)
                v_new = u - _dot(w.astype(BF16), sb)
                vnb = v_new.astype(BF16)
                outs[cc] = _dot((qh * ecol).astype(BF16), sb) + _dot(qk.astype(BF16), vnb)
                kdt = kt * jnp.exp(tot11 - crow)
                s = s * jnp.exp(tot11) + _dot(kdt.astype(BF16), vnb)
            s_scr[h] = s
            o_ref[0, pl.ds(row0, 128), hs] = jnp.concatenate(outs, axis=0)
        return carry

    lax.fori_loop(0, npairs, pair_body, 0)

    @pl.when(i == nblk - 1)
    def _():
        sfin_ref[0] = s_scr[...]


def gdn_scan(feat, gates, cst, s0, reverse, tb=512):
    b, l, _ = feat.shape
    tb = min(tb, l)
    nblk = l // tb
    gate_col = gates.shape[2] // LANES - 1

    def blk(i):
        return (nblk - 1 - i) if reverse else i

    def tok(col):
        return lambda bb, i: (bb, blk(i), col)

    st_spec = pl.BlockSpec((1, GDN_HEADS, HEAD_DIM, HEAD_DIM), lambda bb, i: (bb, 0, 0, 0))
    return pl.pallas_call(
        functools.partial(_gdn_kernel, reverse),
        grid=(b, nblk),
        in_specs=[pl.BlockSpec((1, tb, GDN_W), tok(0)),
                  pl.BlockSpec((1, tb, GDN_W), tok(1)),
                  pl.BlockSpec((1, tb, GDN_W), tok(2)),
                  pl.BlockSpec((1, tb, LANES), tok(gate_col)),
                  pl.BlockSpec((8, LANES), lambda bb, i: (0, 0)),
                  st_spec],
        out_specs=[pl.BlockSpec((1, tb, GDN_W), tok(0)), st_spec],
        out_shape=[jax.ShapeDtypeStruct((b, l, GDN_W), F32),
                   jax.ShapeDtypeStruct((b, GDN_HEADS, HEAD_DIM, HEAD_DIM), F32)],
        scratch_shapes=[pltpu.VMEM((GDN_HEADS, HEAD_DIM, HEAD_DIM), F32),
                        pltpu.VMEM((tb, LANES), F32),
                        pltpu.VMEM((tb, LANES), F32),
                        pltpu.VMEM((tb, LANES), F32),
                        pltpu.VMEM((tb // 128, LANES, 128), F32)],
        compiler_params=_params("parallel", "arbitrary"),
        name="gdn_scan_bwd" if reverse else "gdn_scan_fwd",
    )(feat, feat, feat, gates, cst, s0)


def _gdn_out_kernel(of_ref, ob_ref, z_ref, g_ref, o_ref):
    o = of_ref[0] + ob_ref[0]
    o_ref[0] = (_rms(o, g_ref[...]) * _silu(z_ref[0])).astype(BF16)


def gdn_output(o_f, o_b, proj, norm_w, ts=512):
    b, l, w = o_f.shape
    ts = min(ts, l)
    z_col0 = 3 * GDN_HEADS
    tok = lambda bb, i, h: (bb, i, h)
    return pl.pallas_call(
        _gdn_out_kernel,
        grid=(b, l // ts, GDN_HEADS),
        in_specs=[pl.BlockSpec((1, ts, HEAD_DIM), tok),
                  pl.BlockSpec((1, ts, HEAD_DIM), tok),
                  pl.BlockSpec((1, ts, HEAD_DIM), lambda bb, i, h: (bb, i, z_col0 + h)),
                  pl.BlockSpec((1, HEAD_DIM), lambda bb, i, h: (0, 0))],
        out_specs=pl.BlockSpec((1, ts, HEAD_DIM), tok),
        out_shape=jax.ShapeDtypeStruct((b, l, w), BF16),
        compiler_params=_params("parallel", "parallel", "parallel"),
        name="gdn_output",
    )(o_f, o_b, proj, norm_w.reshape(1, HEAD_DIM))


def _dot_nt(a, b):
    return lax.dot_general(a, b, (((1,), (1,)), ((), ())), preferred_element_type=F32)


def _swa_kernel(windowed, sink_ref, q_ref, *refs):
    if windowed:
        kp_ref, kc_ref, kn_ref, vp_ref, vc_ref, vn_ref, kx_ref, vx_ref, o_ref = refs
    else:
        kx_ref, vx_ref, o_ref = refs
    hk = pl.program_id(1)
    i = pl.program_id(2)
    nb = pl.num_programs(2)
    scale = HEAD_DIM ** -0.5
    kx = kx_ref[0]
    vx = vx_ref[0]
    if windowed:
        kw = jnp.concatenate([kp_ref[0], kc_ref[0], kn_ref[0]], axis=0)
        vw = jnp.concatenate([vp_ref[0], vc_ref[0], vn_ref[0]], axis=0)
        r = lax.broadcasted_iota(jnp.int32, (SWA_BLOCK, 3 * SWA_BLOCK), 0)
        c = lax.broadcasted_iota(jnp.int32, (SWA_BLOCK, 3 * SWA_BLOCK), 1)
        mask = jnp.abs(c - SWA_BLOCK - r) <= SWA_WINDOW
        mask = jnp.logical_and(mask, jnp.logical_or(i > 0, c >= SWA_BLOCK))
        mask = jnp.logical_and(mask, jnp.logical_or(i < nb - 1, c < 2 * SWA_BLOCK))
    for g in range(SWA_GROUP):
        q = q_ref[0, :, g * HEAD_DIM:(g + 1) * HEAD_DIM]
        sink = sink_ref[hk * SWA_GROUP + g]
        s_ctx = _dot_nt(q, kx) * scale
        m = jnp.maximum(jnp.max(s_ctx, axis=-1, keepdims=True), sink)
        if windowed:
            s_win = jnp.where(mask, _dot_nt(q, kw) * scale, -jnp.inf)
            m = jnp.maximum(m, jnp.max(s_win, axis=-1, keepdims=True))
        e_ctx = jnp.exp(s_ctx - m)
        den = jnp.sum(e_ctx, axis=-1, keepdims=True) + jnp.exp(sink - m)
        if windowed:
            e_win = jnp.exp(s_win - m)
            den = den + jnp.sum(e_win, axis=-1, keepdims=True)
        inv = 1.0 / den
        o = _dot((e_ctx * inv).astype(BF16), vx)
        if windowed:
            o = o + _dot((e_win * inv).astype(BF16), vw)
        o_ref[0, :, g * HEAD_DIM:(g + 1) * HEAD_DIM] = o.astype(BF16)


def swa_attention(q, k, v, k_ctx, v_ctx, sink, windowed):
    b, s, _ = q.shape
    nb = s // SWA_BLOCK
    if windowed:
        (k_arr, k_col), (v_arr, v_col) = k, v
    (kx_arr, kx_col), (vx_arr, vx_col) = k_ctx, v_ctx
    n_ctx = kx_arr.shape[1]
    qspec = pl.BlockSpec((1, SWA_BLOCK, SWA_GROUP * HEAD_DIM), lambda bb, hk, i: (bb, i, hk))

    def band(col, shift):
        return pl.BlockSpec((1, SWA_BLOCK, HEAD_DIM),
                            lambda bb, hk, i: (bb, jnp.clip(i + shift, 0, nb - 1), col + hk))

    def ctx(col):
        return pl.BlockSpec((1, n_ctx, HEAD_DIM), lambda bb, hk, i: (bb, 0, col + hk))

    specs = [pl.BlockSpec(memory_space=pltpu.SMEM), qspec]
    args = [sink, q]
    if windowed:
        specs += [band(k_col, -1), band(k_col, 0), band(k_col, 1),
                  band(v_col, -1), band(v_col, 0), band(v_col, 1)]
        args += [k_arr] * 3 + [v_arr] * 3
    specs += [ctx(kx_col), ctx(vx_col)]
    args += [kx_arr, vx_arr]
    return pl.pallas_call(
        functools.partial(_swa_kernel, windowed),
        grid=(b, SWA_KV_HEADS, nb),
        in_specs=specs,
        out_specs=qspec,
        out_shape=jax.ShapeDtypeStruct((b, s, SWA_HEADS * HEAD_DIM), BF16),
        compiler_params=_params("parallel", "parallel", "parallel"),
        name="swa_attention",
    )(*args)


def _diff_kernel(tk, lam_init, lam_ref, nrm_ref, q_ref, k_ref, v_ref, o_ref):
    tq = q_ref.shape[1]
    nk = k_ref.shape[1] // tk
    dq = DIFF_QK_DIM
    q0 = q_ref[0, :, 0:dq]
    q1 = q_ref[0, :, dq:2 * dq]

    def body(j, carry):
        m0, l0, a0, m1, l1, a1 = carry
        k0 = k_ref[0, pl.ds(pl.multiple_of(j * tk, tk), tk), 0:dq]
        k1 = k_ref[0, pl.ds(pl.multiple_of(j * tk, tk), tk), dq:2 * dq]
        v = v_ref[0, pl.ds(pl.multiple_of(j * tk, tk), tk), :]

        def upd(q, k, m, l, acc):
            s = _dot_nt(q, k)
            m_new = jnp.maximum(m, jnp.max(s, axis=-1, keepdims=True))
            alpha = jnp.exp(m - m_new)
            e = jnp.exp(s - m_new)
            l = alpha * l + jnp.sum(e, axis=-1, keepdims=True)
            acc = alpha * acc + _dot(e.astype(BF16), v)
            return m_new, l, acc

        m0, l0, a0 = upd(q0, k0, m0, l0, a0)
        m1, l1, a1 = upd(q1, k1, m1, l1, a1)
        return m0, l0, a0, m1, l1, a1

    neg = jnp.full((tq, 1), -jnp.inf, F32)
    zero1 = jnp.zeros((tq, 1), F32)
    zacc = jnp.zeros((tq, HEAD_DIM), F32)
    m0, l0, a0, m1, l1, a1 = lax.fori_loop(0, nk, body, (neg, zero1, zacc, neg, zero1, zacc))
    lam = lam_ref[...]
    lam_full = (jnp.exp(jnp.sum(lam[0:1] * lam[1:2], axis=-1, keepdims=True))
                - jnp.exp(jnp.sum(lam[2:3] * lam[3:4], axis=-1, keepdims=True)) + lam_init)
    o = a0 / l0 - lam_full * (a1 / l1)
    o_ref[0] = (_rms(o, nrm_ref[...]) * (1.0 - lam_init)).astype(BF16)


def diff_attention(q, k, v, lam, norm_w, lam_init, tq=256, tk=1280):
    (q_arr, q_col), (k_arr, k_col), (v_arr, v_col) = q, k, v
    b, s, _ = q_arr.shape
    sk = k_arr.shape[1]
    tq = min(tq, s)
    tk = min(tk, sk)
    assert s % tq == 0 and sk % tk == 0, (s, sk, tq, tk)
    return pl.pallas_call(
        functools.partial(_diff_kernel, tk, lam_init),
        grid=(b, DIFF_HEADS, s // tq),
        in_specs=[pl.BlockSpec((4, DIFF_QK_DIM), lambda bb, h, i: (0, 0)),
                  pl.BlockSpec((1, HEAD_DIM), lambda bb, h, i: (0, 0)),
                  pl.BlockSpec((1, tq, HEAD_DIM), lambda bb, h, i: (bb, i, q_col + h)),
                  pl.BlockSpec((1, sk, HEAD_DIM), lambda bb, h, i: (bb, 0, k_col + h)),
                  pl.BlockSpec((1, sk, HEAD_DIM), lambda bb, h, i: (bb, 0, v_col + h))],
        out_specs=pl.BlockSpec((1, tq, HEAD_DIM), lambda bb, h, i: (bb, i, h)),
        out_shape=jax.ShapeDtypeStruct((b, s, DIFF_HEADS * HEAD_DIM), BF16),
        compiler_params=_params("parallel", "parallel", "parallel"),
        name="diff_attention",
    )(lam, norm_w.reshape(1, HEAD_DIM), q_arr, k_arr, v_arr)


_O_QKV, _O_Z, _O_BETA, _O_ALPHA, _O_SQ, _O_SK, _O_SV, _O_DQ, _O_DK, _O_DV, _O_END = (
    0, 2304, 3072, 3084, 3096, 3864, 4120, 4376, 4888, 5400, 5912)


def _split_w_in(w):
    gates = jnp.zeros((w.shape[0], LANES), w.dtype).at[:, :_O_SQ - _O_BETA].set(w[:, _O_BETA:_O_SQ])
    w_gdn = jnp.concatenate([w[:, _O_QKV:_O_BETA], gates], axis=1)
    dq = w[:, _O_DQ:_O_DK] * (DIFF_QK_DIM ** -0.5)
    w_qk = jnp.concatenate([w[:, _O_SQ:_O_SV], dq, w[:, _O_DK:_O_DV]], axis=1)
    w_v = jnp.concatenate([w[:, _O_SV:_O_DQ], w[:, _O_DV:_O_END]], axis=1)
    return w_gdn.astype(BF16), w_qk.astype(BF16), w_v.astype(BF16)


def _project(h, w_gdn, w_qk, w_v, qk_dtype):
    b, s, d = h.shape
    h2 = h.reshape(b * s, d)
    p_gdn = matmul(h2, w_gdn, F32, tn=640, name="proj_gdn").reshape(b, s, -1)
    p_qk = matmul(h2, w_qk, qk_dtype, name="proj_qk").reshape(b, s, -1)
    p_v = matmul(h2, w_v, BF16, tn=768, name="proj_v").reshape(b, s, -1)
    return p_gdn, p_qk, p_v


def _gdn_consts(a_log, dt_bias):
    cst = jnp.zeros((8, LANES), F32)
    cst = cst.at[0, 2 * GDN_HEADS:4 * GDN_HEADS].set(-jnp.exp(a_log.astype(F32)).reshape(-1))
    cst = cst.at[1, 2 * GDN_HEADS:4 * GDN_HEADS].set(dt_bias.astype(F32).reshape(-1))
    return cst


def _layer(x, ctx, h_lat, h_ctx, m_lat, m_ctx, p, layer, with_ctx, rope_tabs, next_norm):
    b, s, d = x.shape
    n_ctx = ctx.shape[1]
    w_gdn, w_qk, w_v = _split_w_in(p["w_in"])
    w_out = p["w_out"].astype(BF16)
    lg, lqk, lv = _project(h_lat, w_gdn, w_qk, w_v, F32)
    cg, cqk, cv = _project(h_ctx, w_gdn, w_qk, w_v, BF16)

    cst = _gdn_consts(p["gdn_a_log"], p["gdn_dt_bias"])
    f_lat = gdn_features(lg, p["gdn_conv"])
    f_ctx = gdn_features(cg, p["gdn_conv"])
    s0 = jnp.zeros((b, GDN_HEADS, HEAD_DIM, HEAD_DIM), F32)
    o_lat, o_ctx = [], []
    for reverse in (False, True):
        oc, s_ctx = gdn_scan(f_ctx, cg, cst, s0, reverse)
        ol, _ = gdn_scan(f_lat, lg, cst, s_ctx, reverse)
        o_lat.append(ol)
        o_ctx.append(oc)
    a_lat = gdn_output(o_lat[0], o_lat[1], lg, p["gdn_norm"])

    lqk = rope_qk(lqk, *rope_tabs)
    sink = p["swa_sink"].astype(F32)
    sq_c, sk_c, dq_c, dk_c = 0, SWA_HEADS, SWA_HEADS + SWA_KV_HEADS, SWA_HEADS + SWA_KV_HEADS + DIFF_HEADS
    sv_c, dv_c = 0, SWA_KV_HEADS
    b_lat = swa_attention(lqk, (lqk, sk_c), (lv, sv_c), (cqk, sk_c), (cv, sv_c), sink, True)
    lam_init = 0.8 - 0.6 * math.exp(-0.3 * layer)
    lam = p["diff_lambda"].astype(F32)
    dk_all = jnp.concatenate([lqk[:, :, dk_c * LANES:], cqk[:, :, dk_c * LANES:]], axis=1)
    dv_all = jnp.concatenate([lv[:, :, dv_c * LANES:], cv[:, :, dv_c * LANES:]], axis=1)
    c_lat = diff_attention((lqk, dq_c), (dk_all, 0), (dv_all, 0), lam, p["diff_norm"], lam_init)
    mix = jnp.concatenate([a_lat, b_lat, c_lat], axis=-1).reshape(b * s, d)
    y_lat = matmul(mix, w_out, F32, name="proj_out").reshape(b, s, d)

    x, hf_lat = norm_step(x, resid=(y_lat, m_lat[2], p["norm_post_mix"]),
                          norm=(p["norm_pre_ffn"], m_lat[3], m_lat[4]))
    if with_ctx:
        a_ctx = gdn_output(o_ctx[0], o_ctx[1], cg, p["gdn_norm"])
        b_ctx = swa_attention(cqk, None, None, (cqk, sk_c), (cv, sv_c), sink, False)
        c_ctx = diff_attention((cqk, dq_c), (cqk, dk_c), (cv, dv_c), lam, p["diff_norm"], lam_init,
                               tk=n_ctx)
        mix_c = jnp.concatenate([a_ctx, b_ctx, c_ctx], axis=-1).reshape(b * n_ctx, d)
        y_ctx = matmul(mix_c, w_out, F32, name="proj_out").reshape(b, n_ctx, d)
        ctx, hf_ctx = norm_step(ctx, resid=(y_ctx, m_ctx[2], p["norm_post_mix"]),
                                norm=(p["norm_pre_ffn"], m_ctx[3], m_ctx[4]))

    w_gate = p["ffn_w_gate"].astype(BF16)
    w_up = p["ffn_w_up"].astype(BF16)
    w_down = p["ffn_w_down"].astype(BF16)

    def ffn(h, n_tok):
        h2 = h.reshape(b * n_tok, d)
        g = matmul(h2, w_gate, BF16, name="ffn_gate").reshape(b, n_tok, -1)
        u = matmul(h2, w_up, BF16, name="ffn_up").reshape(b, n_tok, -1)
        act = ffn_act(g, u, p["ffn_conv"]).reshape(b * n_tok, -1)
        return matmul(act, w_down, F32, name="ffn_down").reshape(b, n_tok, d)

    f_lat = ffn(hf_lat, s)
    x, h_lat = norm_step(x, resid=(f_lat, m_lat[5], p["norm_post_ffn"]),
                         norm=None if next_norm is None else next_norm[0])
    h_ctx = None
    if with_ctx:
        f_ctx = ffn(hf_ctx, n_ctx)
        ctx, h_ctx = norm_step(ctx, resid=(f_ctx, m_ctx[5], p["norm_post_ffn"]),
                               norm=None if next_norm is None else next_norm[1])
    return x, ctx, h_lat, h_ctx


def kernel(x, c, ctx, c_ctx, w_mod, b_mod, norm_pre_mix, norm_post_mix, norm_pre_ffn, norm_post_ffn,
           w_in, gdn_conv, gdn_a_log, gdn_dt_bias, gdn_norm, swa_sink, diff_lambda, diff_norm, w_out,
           ffn_w_gate, ffn_w_up, ffn_conv, ffn_w_down):
    b, s, d = x.shape
    depth = w_mod.shape[0]
    rope_tabs = (_rope_tables(s, HEAD_DIM), _rope_tables(s, DIFF_QK_DIM))

    cc = jnp.zeros((8, d), F32).at[:b].set(c).at[b].set(c_ctx)
    mods = []
    for layer in range(depth):
        m = modulation(cc, w_mod[layer], b_mod[layer])
        m_lat = [m[:b, None, j * d:(j + 1) * d] for j in range(6)]
        m_ctx = [jnp.broadcast_to(m[b:b + 1, None, j * d:(j + 1) * d], (b, 1, d)) for j in range(6)]
        mods.append((m_lat, m_ctx))

    def pre_mix(layer):
        m_lat, m_ctx = mods[layer]
        return ((norm_pre_mix[layer], m_lat[0], m_lat[1]), (norm_pre_mix[layer], m_ctx[0], m_ctx[1]))

    n0 = pre_mix(0)
    _, h_lat = norm_step(x, norm=n0[0])
    _, h_ctx = norm_step(ctx, norm=n0[1])
    for layer in range(depth):
        with_ctx = layer < depth - 1
        p = dict(w_in=w_in[layer], gdn_conv=gdn_conv[layer], gdn_a_log=gdn_a_log[layer],
                 gdn_dt_bias=gdn_dt_bias[layer], gdn_norm=gdn_norm[layer], swa_sink=swa_sink[layer],
                 diff_lambda=diff_lambda[layer], diff_norm=diff_norm[layer], w_out=w_out[layer],
                 norm_post_mix=norm_post_mix[layer], norm_pre_ffn=norm_pre_ffn[layer],
                 norm_post_ffn=norm_post_ffn[layer], ffn_w_gate=ffn_w_gate[layer],
                 ffn_w_up=ffn_w_up[layer], ffn_conv=ffn_conv[layer], ffn_w_down=ffn_w_down[layer])
        m_lat, m_ctx = mods[layer]
        next_norm = pre_mix(layer + 1) if layer + 1 < depth else None
        x, ctx, h_lat, h_ctx = _layer(x, ctx, h_lat, h_ctx, m_lat, m_ctx, p, layer, with_ctx,
                                      rope_tabs, next_norm)
    return x
```

```python
import functools
import math

import jax
import jax.numpy as jnp
import numpy as np
from jax import lax
from jax.experimental import pallas as pl
from jax.experimental.pallas import tpu as pltpu

F32 = jnp.float32
BF16 = jnp.bfloat16

GRID_W = 64
HEAD_DIM = 128
LANES = 128
EPS = 1e-6
ROPE_BASE = 10000.0

GDN_HEADS = 6
GDN_CONV = 5
GDN_CHUNK = 64
GDN_W = GDN_HEADS * HEAD_DIM
SWA_HEADS = 6
SWA_KV_HEADS = 2
SWA_GROUP = 3
SWA_WINDOW = 128
SWA_BLOCK = 128
DIFF_HEADS = 4
DIFF_QK_DIM = 64
FFN_CONV = 3

VMEM_LIMIT_BYTES = 56 * 1024 * 1024


def _params(*sem):
    return pltpu.CompilerParams(dimension_semantics=sem, vmem_limit_bytes=VMEM_LIMIT_BYTES)


def _silu(x):
    return x * jax.nn.sigmoid(x)


def _mm_kernel(a_ref, b_ref, o_ref):
    o_ref[...] = jnp.dot(a_ref[...], b_ref[...], preferred_element_type=F32).astype(o_ref.dtype)


def matmul(a, b, out_dtype, tm=512, tn=512, name="matmul"):
    m, k = a.shape
    n = b.shape[1]
    tm = min(tm, m)
    tn = min(tn, n)
    assert m % tm == 0 and n % tn == 0, (m, n, tm, tn)
    return pl.pallas_call(
        _mm_kernel,
        grid=(m // tm, n // tn),
        in_specs=[pl.BlockSpec((tm, k), lambda i, j: (i, 0)),
                  pl.BlockSpec((k, tn), lambda i, j: (0, j))],
        out_specs=pl.BlockSpec((tm, tn), lambda i, j: (i, j)),
        out_shape=jax.ShapeDtypeStruct((m, n), out_dtype),
        compiler_params=_params("parallel", "parallel"),
        name=name,
    )(a, b)


def _mod_kernel(c_ref, w_ref, b_ref, o_ref):
    x = _silu(c_ref[...]).astype(BF16)
    o_ref[...] = jnp.dot(x, w_ref[...].astype(BF16), preferred_element_type=F32) + b_ref[...]


def modulation(cc, w, b):
    m, k = cc.shape
    n = w.shape[1]
    tn = 1024
    return pl.pallas_call(
        _mod_kernel,
        grid=(n // tn,),
        in_specs=[pl.BlockSpec((m, k), lambda j: (0, 0)),
                  pl.BlockSpec((k, tn), lambda j: (0, j)),
                  pl.BlockSpec((1, tn), lambda j: (0, j))],
        out_specs=pl.BlockSpec((m, tn), lambda j: (0, j)),
        out_shape=jax.ShapeDtypeStruct((m, n), F32),
        compiler_params=_params("parallel"),
        name="modulation",
    )(cc, w, b.reshape(1, n))


def _rms(x, g):
    return x * lax.rsqrt(jnp.mean(x * x, axis=-1, keepdims=True) + EPS) * g


def _norm_kernel(has_resid, has_norm, *refs):
    refs = list(refs)
    x = refs.pop(0)[0]
    if has_resid:
        y = refs.pop(0)[0]
        gate = refs.pop(0)[0]
        gpost = refs.pop(0)[...]
        x = x + gate * _rms(y, gpost)
    if has_norm:
        gpre = refs.pop(0)[...]
        shift = refs.pop(0)[0]
        scale = refs.pop(0)[0]
    if has_resid:
        refs.pop(0)[0] = x
    if has_norm:
        refs.pop(0)[0] = (_rms(x, gpre) * (1.0 + scale) + shift).astype(BF16)


def norm_step(x, resid=None, norm=None, ts=512):
    b, s, d = x.shape
    ts = min(ts, s)
    assert s % ts == 0
    tok = pl.BlockSpec((1, ts, d), lambda i, j: (i, j, 0))
    vec = pl.BlockSpec((1, 1, d), lambda i, j: (i, 0, 0))
    gsp = pl.BlockSpec((1, d), lambda i, j: (0, 0))
    args, specs, outs, out_specs = [x], [tok], [], []
    if resid is not None:
        y, gate, gpost = resid
        args += [y, gate, gpost.reshape(1, d)]
        specs += [tok, vec, gsp]
        outs.append(jax.ShapeDtypeStruct((b, s, d), F32))
        out_specs.append(tok)
    if norm is not None:
        gpre, shift, scale = norm
        args += [gpre.reshape(1, d), shift, scale]
        specs += [gsp, vec, vec]
        outs.append(jax.ShapeDtypeStruct((b, s, d), BF16))
        out_specs.append(tok)
    res = pl.pallas_call(
        functools.partial(_norm_kernel, resid is not None, norm is not None),
        grid=(b, s // ts),
        in_specs=specs, out_specs=out_specs, out_shape=outs,
        compiler_params=_params("parallel", "parallel"),
        name="norm_step",
    )(*args)
    res = list(res)
    x_new = res.pop(0) if resid is not None else None
    h = res.pop(0) if norm is not None else None
    return x_new, h


def _seq_conv(x, prev8, next8, w, halo):
    t = x.shape[0]
    xe = jnp.concatenate([prev8, x, next8], axis=0)
    acc = None
    for j in range(2 * halo + 1):
        off = 8 + j - halo
        term = xe[off:off + t] * w[j:j + 1]
        acc = term if acc is None else acc + term
    return acc


def _halo_specs(ts, rows, width, col_of):
    per = ts // rows

    def main(b, i, c):
        return (b, i, col_of(c))

    def prev(b, i, c):
        return (b, jnp.maximum(i * per - 1, 0), col_of(c))

    def nxt(nblk):
        def f(b, i, c):
            return (b, jnp.minimum((i + 1) * per, nblk * per - 1), col_of(c))
        return f

    return main, prev, nxt


def _ffn_act_kernel(g_ref, gp_ref, gn_ref, u_ref, w_ref, o_ref):
    i = pl.program_id(1)
    last = pl.num_programs(1) - 1
    g = g_ref[0].astype(F32)
    prev8 = gp_ref[0].astype(F32)[8:16] * (i > 0).astype(F32)
    next8 = gn_ref[0].astype(F32)[0:8] * (i < last).astype(F32)
    y = _seq_conv(g, prev8, next8, w_ref[...], FFN_CONV // 2)
    o_ref[0] = (_silu(y) * u_ref[0].astype(F32)).astype(BF16)


def ffn_act(g, u, conv_w, ts=512, tc=512):
    b, s, f = g.shape
    ts = min(ts, s)
    nblk = s // ts
    main, prev, nxt = _halo_specs(ts, 16, tc, lambda c: c)
    wpad = jnp.zeros((8, f), F32).at[:FFN_CONV].set(conv_w)
    return pl.pallas_call(
        _ffn_act_kernel,
        grid=(b, nblk, f // tc),
        in_specs=[pl.BlockSpec((1, ts, tc), main),
                  pl.BlockSpec((1, 16, tc), prev),
                  pl.BlockSpec((1, 16, tc), nxt(nblk)),
                  pl.BlockSpec((1, ts, tc), main),
                  pl.BlockSpec((8, tc), lambda bb, i, c: (0, c))],
        out_specs=pl.BlockSpec((1, ts, tc), main),
        out_shape=jax.ShapeDtypeStruct((b, s, f), BF16),
        compiler_params=_params("parallel", "parallel", "parallel"),
        name="ffn_act",
    )(g, g, g, u, wpad)


def _rope_tables(n_tokens, dim):
    quarter = dim // 4
    inv_freq = ROPE_BASE ** (-jnp.arange(quarter, dtype=F32) / quarter)
    tok = jnp.arange(n_tokens)
    rows = (tok // GRID_W).astype(F32)
    cols = (tok % GRID_W).astype(F32)
    ang_r = rows[:, None] * inv_freq
    ang_c = cols[:, None] * inv_freq
    ang = jnp.concatenate([ang_r, ang_r, ang_c, ang_c], axis=-1)
    ang = jnp.tile(ang, (1, LANES // dim))
    cos, sin = jnp.cos(ang), jnp.sin(ang)
    first = (jnp.arange(LANES) % (2 * quarter)) < quarter
    sin_a = jnp.where(first, -sin, 0.0)
    sin_b = jnp.where(first, 0.0, sin)
    return jnp.stack([cos, sin_a, sin_b])


def _rope_kernel(x_ref, tb_ref, tc_ref, o_ref):
    j = pl.program_id(2)
    ngroups = x_ref.shape[2] // LANES

    def apply(tab_ref, quarter):
        cos, sin_a, sin_b = tab_ref[0], tab_ref[1], tab_ref[2]
        for gidx in range(ngroups):
            sl = slice(gidx * LANES, (gidx + 1) * LANES)
            x = x_ref[0, :, sl]
            up = pltpu.roll(x, LANES - quarter, axis=1)
            dn = pltpu.roll(x, quarter, axis=1)
            o_ref[0, :, sl] = (x * cos + up * sin_a + dn * sin_b).astype(BF16)

    @pl.when(j == 0)
    def _():
        apply(tb_ref, HEAD_DIM // 4)

    @pl.when(j == 1)
    def _():
        apply(tc_ref, DIFF_QK_DIM // 4)


def rope_qk(x, tab_b, tab_c, ts=512):
    b, s, w = x.shape
    half = w // 2
    ts = min(ts, s)
    return pl.pallas_call(
        _rope_kernel,
        grid=(b, s // ts, 2),
        in_specs=[pl.BlockSpec((1, ts, half), lambda bb, i, j: (bb, i, j)),
                  pl.BlockSpec((3, ts, LANES), lambda bb, i, j: (0, i, 0)),
                  pl.BlockSpec((3, ts, LANES), lambda bb, i, j: (0, i, 0))],
        out_specs=pl.BlockSpec((1, ts, half), lambda bb, i, j: (bb, i, j)),
        out_shape=jax.ShapeDtypeStruct((b, s, w), BF16),
        compiler_params=_params("parallel", "parallel", "parallel"),
        name="rope_qk",
    )(x, tab_b, tab_c)


def _gdn_feat_kernel(x_ref, xp_ref, xn_ref, w_ref, o_ref):
    i = pl.program_id(1)
    c = pl.program_id(2)
    last = pl.num_programs(1) - 1
    prev8 = xp_ref[0] * (i > 0).astype(F32)
    next8 = xn_ref[0] * (i < last).astype(F32)
    y = _silu(_seq_conv(x_ref[0], prev8, next8, w_ref[...], GDN_CONV // 2))
    inv = lax.rsqrt(jnp.sum(y * y, axis=-1, keepdims=True) + EPS)
    kind = c // GDN_HEADS
    inv = jnp.where(kind == 0, inv * (HEAD_DIM ** -0.5), inv)
    inv = jnp.where(kind == 2, jnp.ones_like(inv), inv)
    o_ref[0] = y * inv


def gdn_features(proj, conv_w, ts=512):
    b, l, _ = proj.shape
    ts = min(ts, l)
    nblk = l // ts
    ncol = 3 * GDN_HEADS
    main, prev, nxt = _halo_specs(ts, 8, LANES, lambda c: c)
    wpad = jnp.zeros((8, ncol * LANES), F32).at[:GDN_CONV].set(conv_w)
    return pl.pallas_call(
        _gdn_feat_kernel,
        grid=(b, nblk, ncol),
        in_specs=[pl.BlockSpec((1, ts, LANES), main),
                  pl.BlockSpec((1, 8, LANES), prev),
                  pl.BlockSpec((1, 8, LANES), nxt(nblk)),
                  pl.BlockSpec((8, LANES), lambda bb, i, c: (0, c))],
        out_specs=pl.BlockSpec((1, ts, LANES), main),
        out_shape=jax.ShapeDtypeStruct((b, l, ncol * LANES), F32),
        compiler_params=_params("parallel", "parallel", "parallel"),
        name="gdn_features",
    )(proj, proj, proj, wpad)


def _split3(x):
    hi = x.astype(BF16)
    r = x - hi.astype(F32)
    mid = r.astype(BF16)
    lo = (r - mid.astype(F32)).astype(BF16)
    return hi, mid, lo


def _dot(a, b):
    return jnp.dot(a, b, preferred_element_type=F32)


def _mask_dot(mask_bf16, x):
    hi, mid, lo = _split3(x)
    return _dot(mask_bf16, hi) + _dot(mask_bf16, mid) + _dot(mask_bf16, lo)


def _split2(x):
    hi = x.astype(BF16)
    lo = (x - hi.astype(F32)).astype(BF16)
    return hi, lo


def _dot3(a, b):
    return _dot(a[0], b[0]) + _dot(a[0], b[1]) + _dot(a[1], b[0])


def _unit_tri_inverse(a_list, same16, same32):
    n = a_list[0].shape[0]
    eye = (lax.broadcasted_iota(jnp.int32, (n, n), 0) == lax.broadcasted_iota(jnp.int32, (n, n), 1)).astype(F32)
    off16 = jnp.logical_and(same32, jnp.logical_not(same16))
    d = [jnp.where(same16, a, 0.0) for a in a_list]
    l1s = [_split2(jnp.where(off16, a, 0.0)) for a in a_list]
    l2s = [_split2(jnp.where(same32, 0.0, a)) for a in a_list]
    ds = [_split2(x) for x in d]
    d2 = [_dot3(x, x) for x in ds]
    d2s = [_split2(x) for x in d2]
    d3 = [_dot3(x2, x) for x2, x in zip(d2s, ds)]
    d4s = [_split2(_dot3(x2, x2)) for x2 in d2s]
    p = [eye - x + x2 - x3 for x, x2, x3 in zip(d, d2, d3)]
    ps = [_split2(x) for x in p]
    p = [x + _dot3(xs, x4) for x, xs, x4 in zip(p, ps, d4s)]
    d8s = [_split2(_dot3(x4, x4)) for x4 in d4s]
    ps = [_split2(x) for x in p]
    t = [x + _dot3(xs, x8) for x, xs, x8 in zip(p, ps, d8s)]
    for ls in (l1s, l2s):
        ts = [_split2(x) for x in t]
        xs = [_split2(_dot3(l, y)) for l, y in zip(ls, ts)]
        t = [x - _dot3(y, z) for x, y, z in zip(t, ts, xs)]
    return [_split2(x) for x in t]


def _gdn_kernel(reverse, q_ref, k_ref, v_ref, gt_ref, cst_ref, s0_ref, o_ref, sfin_ref,
                s_scr, g_scr, cum_scr, tot_scr, xt_scr):
    i = pl.program_id(1)
    nblk = pl.num_programs(1)
    tb = q_ref.shape[1]
    npairs = tb // 128
    d = 1 if reverse else 0
    heads = range(GDN_HEADS)

    @pl.when(i == 0)
    def _():
        s_scr[...] = s0_ref[0]

    raw = gt_ref[0]
    lane = lax.broadcasted_iota(jnp.int32, raw.shape, 1)
    neg_a = cst_ref[0:1, :]
    dt_bias = cst_ref[1:2, :]
    gates = jnp.where(lane < 2 * GDN_HEADS, jax.nn.sigmoid(raw), neg_a * jax.nn.softplus(raw + dt_bias))
    g_scr[...] = gates

    r64 = lax.broadcasted_iota(jnp.int32, (GDN_CHUNK, GDN_CHUNK), 0)
    c64 = lax.broadcasted_iota(jnp.int32, (GDN_CHUNK, GDN_CHUNK), 1)
    incl = (r64 <= c64) if reverse else (r64 >= c64)
    strict = (r64 < c64) if reverse else (r64 > c64)
    same16 = (r64 // 16) == (c64 // 16)
    same32 = (r64 // 32) == (c64 // 32)

    r128 = lax.broadcasted_iota(jnp.int32, (128, 128), 0)
    c128 = lax.broadcasted_iota(jnp.int32, (128, 128), 1)
    same_chunk = (r128 // GDN_CHUNK) == (c128 // GDN_CHUNK)
    order = (r128 <= c128) if reverse else (r128 >= c128)
    cum_mask = jnp.logical_and(same_chunk, order).astype(BF16)
    tot_mask = same_chunk.astype(BF16)
    for p in range(npairs):
        rows = slice(p * 128, (p + 1) * 128)
        gp = gates[rows]
        cum = _mask_dot(cum_mask, gp)
        tot = _mask_dot(tot_mask, gp)
        cum_scr[rows, :] = cum
        tot_scr[rows, :] = tot
        lane_p = lax.broadcasted_iota(jnp.int32, gp.shape, 1)
        xt_scr[p] = jnp.where(lane_p < 2 * GDN_HEADS, gp, cum).T

    chunk_order = (1, 0) if reverse else (0, 1)

    def pair_body(pp, carry):
        p = (npairs - 1 - pp) if reverse else pp
        row0 = pl.multiple_of(p * 128, 128)
        gp = g_scr[pl.ds(row0, 128), :]
        cump = cum_scr[pl.ds(row0, 128), :]
        totp = tot_scr[pl.ds(row0, 128), :]
        xtp = xt_scr[p]
        ecum = jnp.exp(cump)

        units = [(h, cc) for cc in chunk_order for h in heads]
        hsl = [slice(h * HEAD_DIM, (h + 1) * HEAD_DIM) for h in heads]
        qp = [q_ref[0, pl.ds(row0, 128), hsl[h]] for h in heads]
        kp = [k_ref[0, pl.ds(row0, 128), hsl[h]] for h in heads]
        vp = [v_ref[0, pl.ds(row0, 128), hsl[h]] for h in heads]
        ktp = [x.T for x in kp]

        def rs(cc):
            return slice(cc * GDN_CHUNK, (cc + 1) * GDN_CHUNK)

        def col(arr, cc, j):
            return arr[rs(cc), j:j + 1]

        bi = [d * GDN_HEADS + h for h in heads]
        gi = [2 * GDN_HEADS + d * GDN_HEADS + h for h in heads]
        bcol = [col(gp, cc, bi[h]) for h, cc in units]
        ecol = [col(ecum, cc, gi[h]) for h, cc in units]
        crow = [xtp[gi[h]:gi[h] + 1, rs(cc)] for h, cc in units]
        tot11 = [totp[cc * GDN_CHUNK:cc * GDN_CHUNK + 1, gi[h]:gi[h] + 1] for h, cc in units]
        decay = [jnp.exp(jnp.where(incl, col(cump, cc, gi[h]) - cr, -jnp.inf))
                 for (h, cc), cr in zip(units, crow)]
        kh = [kp[h][rs(cc)] for h, cc in units]
        qh = [qp[h][rs(cc)] for h, cc in units]
        vh = [vp[h][rs(cc)] for h, cc in units]
        kt = [ktp[h][:, rs(cc)] for h, cc in units]
        ktb = [x.astype(BF16) for x in kt]
        kk = [_dot(x.astype(BF16), y) for x, y in zip(kh, ktb)]
        qk = [_dot(x.astype(BF16), y) * dc for x, y, dc in zip(qh, ktb, decay)]
        a = [jnp.where(strict, b_ * x * dc, 0.0) for b_, x, dc in zip(bcol, kk, decay)]
        ts = _unit_tri_inverse(a, same16, same32)
        rhs = [_split2(jnp.concatenate([b_ * v_, (b_ * e_) * k_], axis=1))
               for b_, e_, v_, k_ in zip(bcol, ecol, vh, kh)]
        uw = [_dot3(t_, r_) for t_, r_ in zip(ts, rhs)]
        lhs_s = [jnp.concatenate([x[:, HEAD_DIM:], q_ * e_], axis=0).astype(BF16)
                 for x, q_, e_ in zip(uw, qh, ecol)]
        lhs_v = [jnp.concatenate([x, y * jnp.exp(t11 - cr)], axis=0).astype(BF16)
                 for x, y, t11, cr in zip(qk, kt, tot11, crow)]
        dec = [jnp.exp(x) for x in tot11]

        s = [s_scr[h] for h in heads]
        outs = {}
        for ci, cc in enumerate(chunk_order):
            un = [ci * GDN_HEADS + h for h in heads]
            sb = [x.astype(BF16) for x in s]
            rs_ = [_dot(lhs_s[j], sb[h]) for h, j in zip(heads, un)]
            vnb = [(uw[j][:, :HEAD_DIM] - r_[:GDN_CHUNK]).astype(BF16) for j, r_ in zip(un, rs_)]
            rv = [_dot(lhs_v[j], x) for j, x in zip(un, vnb)]
            for h, j in zip(heads, un):
                outs[(h, cc)] = rs_[h][GDN_CHUNK:] + rv[h][:GDN_CHUNK]
            s = [x * dec[j] + r_[GDN_CHUNK:] for x, j, r_ in zip(s, un, rv)]
        for h in heads:
            s_scr[h] = s[h]
            o_ref[0, pl.ds(row0, 128), hsl[h]] = jnp.concatenate([outs[(h, 0)], outs[(h, 1)]], axis=0)
        return carry

    lax.fori_loop(0, npairs, pair_body, 0)

    @pl.when(i == nblk - 1)
    def _():
        sfin_ref[0] = s_scr[...]


def gdn_scan(feat, gates, cst, s0, reverse, tb=512):
    b, l, _ = feat.shape
    tb = min(tb, l)
    nblk = l // tb
    gate_col = gates.shape[2] // LANES - 1

    def blk(i):
        return (nblk - 1 - i) if reverse else i

    def tok(col):
        return lambda bb, i: (bb, blk(i), col)

    st_spec = pl.BlockSpec((1, GDN_HEADS, HEAD_DIM, HEAD_DIM), lambda bb, i: (bb, 0, 0, 0))
    return pl.pallas_call(
        functools.partial(_gdn_kernel, reverse),
        grid=(b, nblk),
        in_specs=[pl.BlockSpec((1, tb, GDN_W), tok(0)),
                  pl.BlockSpec((1, tb, GDN_W), tok(1)),
                  pl.BlockSpec((1, tb, GDN_W), tok(2)),
                  pl.BlockSpec((1, tb, LANES), tok(gate_col)),
                  pl.BlockSpec((8, LANES), lambda bb, i: (0, 0)),
                  st_spec],
        out_specs=[pl.BlockSpec((1, tb, GDN_W), tok(0)), st_spec],
        out_shape=[jax.ShapeDtypeStruct((b, l, GDN_W), F32),
                   jax.ShapeDtypeStruct((b, GDN_HEADS, HEAD_DIM, HEAD_DIM), F32)],
        scratch_shapes=[pltpu.VMEM((GDN_HEADS, HEAD_DIM, HEAD_DIM), F32),
                        pltpu.VMEM((tb, LANES), F32),
                        pltpu.VMEM((tb, LANES), F32),
                        pltpu.VMEM((tb, LANES), F32),
                        pltpu.VMEM((tb // 128, LANES, 128), F32)],
        compiler_params=_params("parallel", "arbitrary"),
        name="gdn_scan_bwd" if reverse else "gdn_scan_fwd",
    )(feat, feat, feat, gates, cst, s0)


def _gdn_out_kernel(of_ref, ob_ref, z_ref, g_ref, o_ref):
    o = of_ref[0] + ob_ref[0]
    o_ref[0] = (_rms(o, g_ref[...]) * _silu(z_ref[0])).astype(BF16)


def gdn_output(o_f, o_b, proj, norm_w, ts=512):
    b, l, w = o_f.shape
    ts = min(ts, l)
    z_col0 = 3 * GDN_HEADS
    tok = lambda bb, i, h: (bb, i, h)
    return pl.pallas_call(
        _gdn_out_kernel,
        grid=(b, l // ts, GDN_HEADS),
        in_specs=[pl.BlockSpec((1, ts, HEAD_DIM), tok),
                  pl.BlockSpec((1, ts, HEAD_DIM), tok),
                  pl.BlockSpec((1, ts, HEAD_DIM), lambda bb, i, h: (bb, i, z_col0 + h)),
                  pl.BlockSpec((1, HEAD_DIM), lambda bb, i, h: (0, 0))],
        out_specs=pl.BlockSpec((1, ts, HEAD_DIM), tok),
        out_shape=jax.ShapeDtypeStruct((b, l, w), BF16),
        compiler_params=_params("parallel", "parallel", "parallel"),
        name="gdn_output",
    )(o_f, o_b, proj, norm_w.reshape(1, HEAD_DIM))


def _dot_nt(a, b):
    return lax.dot_general(a, b, (((1,), (1,)), ((), ())), preferred_element_type=F32)


def _swa_kernel(windowed, sink_ref, q_ref, *refs):
    if windowed:
        kp_ref, kc_ref, kn_ref, vp_ref, vc_ref, vn_ref, kx_ref, vx_ref, o_ref = refs
    else:
        kx_ref, vx_ref, o_ref = refs
    hk = pl.program_id(1)
    i = pl.program_id(2)
    nb = pl.num_programs(2)
    scale = HEAD_DIM ** -0.5
    kx = kx_ref[0]
    vx = vx_ref[0]
    if windowed:
        kw = jnp.concatenate([kp_ref[0], kc_ref[0], kn_ref[0]], axis=0)
        vw = jnp.concatenate([vp_ref[0], vc_ref[0], vn_ref[0]], axis=0)
        r = lax.broadcasted_iota(jnp.int32, (SWA_BLOCK, 3 * SWA_BLOCK), 0)
        c = lax.broadcasted_iota(jnp.int32, (SWA_BLOCK, 3 * SWA_BLOCK), 1)
        mask = jnp.abs(c - SWA_BLOCK - r) <= SWA_WINDOW
        mask = jnp.logical_and(mask, jnp.logical_or(i > 0, c >= SWA_BLOCK))
        mask = jnp.logical_and(mask, jnp.logical_or(i < nb - 1, c < 2 * SWA_BLOCK))
    for g in range(SWA_GROUP):
        q = q_ref[0, :, g * HEAD_DIM:(g + 1) * HEAD_DIM]
        sink = sink_ref[hk * SWA_GROUP + g]
        s_ctx = _dot_nt(q, kx) * scale
        m = jnp.maximum(jnp.max(s_ctx, axis=-1, keepdims=True), sink)
        if windowed:
            s_win = jnp.where(mask, _dot_nt(q, kw) * scale, -jnp.inf)
            m = jnp.maximum(m, jnp.max(s_win, axis=-1, keepdims=True))
        e_ctx = jnp.exp(s_ctx - m)
        den = jnp.sum(e_ctx, axis=-1, keepdims=True) + jnp.exp(sink - m)
        if windowed:
            e_win = jnp.exp(s_win - m)
            den = den + jnp.sum(e_win, axis=-1, keepdims=True)
        inv = 1.0 / den
        o = _dot((e_ctx * inv).astype(BF16), vx)
        if windowed:
            o = o + _dot((e_win * inv).astype(BF16), vw)
        o_ref[0, :, g * HEAD_DIM:(g + 1) * HEAD_DIM] = o.astype(BF16)


def swa_attention(q, k, v, k_ctx, v_ctx, sink, windowed):
    b, s, _ = q.shape
    nb = s // SWA_BLOCK
    if windowed:
        (k_arr, k_col), (v_arr, v_col) = k, v
    (kx_arr, kx_col), (vx_arr, vx_col) = k_ctx, v_ctx
    n_ctx = kx_arr.shape[1]
    qspec = pl.BlockSpec((1, SWA_BLOCK, SWA_GROUP * HEAD_DIM), lambda bb, hk, i: (bb, i, hk))

    def band(col, shift):
        return pl.BlockSpec((1, SWA_BLOCK, HEAD_DIM),
                            lambda bb, hk, i: (bb, jnp.clip(i + shift, 0, nb - 1), col + hk))

    def ctx(col):
        return pl.BlockSpec((1, n_ctx, HEAD_DIM), lambda bb, hk, i: (bb, 0, col + hk))

    specs = [pl.BlockSpec(memory_space=pltpu.SMEM), qspec]
    args = [sink, q]
    if windowed:
        specs += [band(k_col, -1), band(k_col, 0), band(k_col, 1),
                  band(v_col, -1), band(v_col, 0), band(v_col, 1)]
        args += [k_arr] * 3 + [v_arr] * 3
    specs += [ctx(kx_col), ctx(vx_col)]
    args += [kx_arr, vx_arr]
    return pl.pallas_call(
        functools.partial(_swa_kernel, windowed),
        grid=(b, SWA_KV_HEADS, nb),
        in_specs=specs,
        out_specs=qspec,
        out_shape=jax.ShapeDtypeStruct((b, s, SWA_HEADS * HEAD_DIM), BF16),
        compiler_params=_params("parallel", "parallel", "parallel"),
        name="swa_attention",
    )(*args)


def _diff_kernel(tk, lam_init, lam_ref, nrm_ref, q_ref, k_ref, v_ref, o_ref):
    tq = q_ref.shape[1]
    nk = k_ref.shape[1] // tk
    dq = DIFF_QK_DIM
    q0 = q_ref[0, :, 0:dq]
    q1 = q_ref[0, :, dq:2 * dq]

    def body(j, carry):
        m0, l0, a0, m1, l1, a1 = carry
        k0 = k_ref[0, pl.ds(pl.multiple_of(j * tk, tk), tk), 0:dq]
        k1 = k_ref[0, pl.ds(pl.multiple_of(j * tk, tk), tk), dq:2 * dq]
        v = v_ref[0, pl.ds(pl.multiple_of(j * tk, tk), tk), :]

        def upd(q, k, m, l, acc):
            s = _dot_nt(q, k)
            m_new = jnp.maximum(m, jnp.max(s, axis=-1, keepdims=True))
            alpha = jnp.exp(m - m_new)
            e = jnp.exp(s - m_new)
            l = alpha * l + jnp.sum(e, axis=-1, keepdims=True)
            acc = alpha * acc + _dot(e.astype(BF16), v)
            return m_new, l, acc

        m0, l0, a0 = upd(q0, k0, m0, l0, a0)
        m1, l1, a1 = upd(q1, k1, m1, l1, a1)
        return m0, l0, a0, m1, l1, a1

    neg = jnp.full((tq, 1), -jnp.inf, F32)
    zero1 = jnp.zeros((tq, 1), F32)
    zacc = jnp.zeros((tq, HEAD_DIM), F32)
    m0, l0, a0, m1, l1, a1 = lax.fori_loop(0, nk, body, (neg, zero1, zacc, neg, zero1, zacc))
    lam = lam_ref[...]
    lam_full = (jnp.exp(jnp.sum(lam[0:1] * lam[1:2], axis=-1, keepdims=True))
                - jnp.exp(jnp.sum(lam[2:3] * lam[3:4], axis=-1, keepdims=True)) + lam_init)
    o = a0 / l0 - lam_full * (a1 / l1)
    o_ref[0] = (_rms(o, nrm_ref[...]) * (1.0 - lam_init)).astype(BF16)


def diff_attention(q, k, v, lam, norm_w, lam_init, tq=256, tk=1280):
    (q_arr, q_col), (k_arr, k_col), (v_arr, v_col) = q, k, v
    b, s, _ = q_arr.shape
    sk = k_arr.shape[1]
    tq = min(tq, s)
    tk = min(tk, sk)
    assert s % tq == 0 and sk % tk == 0, (s, sk, tq, tk)
    return pl.pallas_call(
        functools.partial(_diff_kernel, tk, lam_init),
        grid=(b, DIFF_HEADS, s // tq),
        in_specs=[pl.BlockSpec((4, DIFF_QK_DIM), lambda bb, h, i: (0, 0)),
                  pl.BlockSpec((1, HEAD_DIM), lambda bb, h, i: (0, 0)),
                  pl.BlockSpec((1, tq, HEAD_DIM), lambda bb, h, i: (bb, i, q_col + h)),
                  pl.BlockSpec((1, sk, HEAD_DIM), lambda bb, h, i: (bb, 0, k_col + h)),
                  pl.BlockSpec((1, sk, HEAD_DIM), lambda bb, h, i: (bb, 0, v_col + h))],
        out_specs=pl.BlockSpec((1, tq, HEAD_DIM), lambda bb, h, i: (bb, i, h)),
        out_shape=jax.ShapeDtypeStruct((b, s, DIFF_HEADS * HEAD_DIM), BF16),
        compiler_params=_params("parallel", "parallel", "parallel"),
        name="diff_attention",
    )(lam, norm_w.reshape(1, HEAD_DIM), q_arr, k_arr, v_arr)


_O_QKV, _O_Z, _O_BETA, _O_ALPHA, _O_SQ, _O_SK, _O_SV, _O_DQ, _O_DK, _O_DV, _O_END = (
    0, 2304, 3072, 3084, 3096, 3864, 4120, 4376, 4888, 5400, 5912)


def _split_w_in(w):
    gates = jnp.zeros((w.shape[0], LANES), w.dtype).at[:, :_O_SQ - _O_BETA].set(w[:, _O_BETA:_O_SQ])
    w_gdn = jnp.concatenate([w[:, _O_QKV:_O_BETA], gates], axis=1)
    dq = w[:, _O_DQ:_O_DK] * (DIFF_QK_DIM ** -0.5)
    w_qk = jnp.concatenate([w[:, _O_SQ:_O_SV], dq, w[:, _O_DK:_O_DV]], axis=1)
    w_v = jnp.concatenate([w[:, _O_SV:_O_DQ], w[:, _O_DV:_O_END]], axis=1)
    return w_gdn.astype(BF16), w_qk.astype(BF16), w_v.astype(BF16)


def _project(h, w_gdn, w_qk, w_v, qk_dtype):
    b, s, d = h.shape
    h2 = h.reshape(b * s, d)
    p_gdn = matmul(h2, w_gdn, F32, tn=640, name="proj_gdn").reshape(b, s, -1)
    p_qk = matmul(h2, w_qk, qk_dtype, name="proj_qk").reshape(b, s, -1)
    p_v = matmul(h2, w_v, BF16, tn=768, name="proj_v").reshape(b, s, -1)
    return p_gdn, p_qk, p_v


def _gdn_consts(a_log, dt_bias):
    cst = jnp.zeros((8, LANES), F32)
    cst = cst.at[0, 2 * GDN_HEADS:4 * GDN_HEADS].set(-jnp.exp(a_log.astype(F32)).reshape(-1))
    cst = cst.at[1, 2 * GDN_HEADS:4 * GDN_HEADS].set(dt_bias.astype(F32).reshape(-1))
    return cst


def _layer(x, ctx, h_lat, h_ctx, m_lat, m_ctx, p, layer, with_ctx, rope_tabs, next_norm):
    b, s, d = x.shape
    n_ctx = ctx.shape[1]
    w_gdn, w_qk, w_v = _split_w_in(p["w_in"])
    w_out = p["w_out"].astype(BF16)
    lg, lqk, lv = _project(h_lat, w_gdn, w_qk, w_v, F32)
    cg, cqk, cv = _project(h_ctx, w_gdn, w_qk, w_v, BF16)

    cst = _gdn_consts(p["gdn_a_log"], p["gdn_dt_bias"])
    f_lat = gdn_features(lg, p["gdn_conv"])
    f_ctx = gdn_features(cg, p["gdn_conv"])
    s0 = jnp.zeros((b, GDN_HEADS, HEAD_DIM, HEAD_DIM), F32)
    o_lat, o_ctx = [], []
    for reverse in (False, True):
        oc, s_ctx = gdn_scan(f_ctx, cg, cst, s0, reverse)
        ol, _ = gdn_scan(f_lat, lg, cst, s_ctx, reverse)
        o_lat.append(ol)
        o_ctx.append(oc)
    a_lat = gdn_output(o_lat[0], o_lat[1], lg, p["gdn_norm"])

    lqk = rope_qk(lqk, *rope_tabs)
    sink = p["swa_sink"].astype(F32)
    sq_c, sk_c, dq_c, dk_c = 0, SWA_HEADS, SWA_HEADS + SWA_KV_HEADS, SWA_HEADS + SWA_KV_HEADS + DIFF_HEADS
    sv_c, dv_c = 0, SWA_KV_HEADS
    b_lat = swa_attention(lqk, (lqk, sk_c), (lv, sv_c), (cqk, sk_c), (cv, sv_c), sink, True)
    lam_init = 0.8 - 0.6 * math.exp(-0.3 * layer)
    lam = p["diff_lambda"].astype(F32)
    dk_all = jnp.concatenate([lqk[:, :, dk_c * LANES:], cqk[:, :, dk_c * LANES:]], axis=1)
    dv_all = jnp.concatenate([lv[:, :, dv_c * LANES:], cv[:, :, dv_c * LANES:]], axis=1)
    c_lat = diff_attention((lqk, dq_c), (dk_all, 0), (dv_all, 0), lam, p["diff_norm"], lam_init)
    mix = jnp.concatenate([a_lat, b_lat, c_lat], axis=-1).reshape(b * s, d)
    y_lat = matmul(mix, w_out, F32, name="proj_out").reshape(b, s, d)

    x, hf_lat = norm_step(x, resid=(y_lat, m_lat[2], p["norm_post_mix"]),
                          norm=(p["norm_pre_ffn"], m_lat[3], m_lat[4]))
    if with_ctx:
        a_ctx = gdn_output(o_ctx[0], o_ctx[1], cg, p["gdn_norm"])
        b_ctx = swa_attention(cqk, None, None, (cqk, sk_c), (cv, sv_c), sink, False)
        c_ctx = diff_attention((cqk, dq_c), (cqk, dk_c), (cv, dv_c), lam, p["diff_norm"], lam_init,
                               tk=n_ctx)
        mix_c = jnp.concatenate([a_ctx, b_ctx, c_ctx], axis=-1).reshape(b * n_ctx, d)
        y_ctx = matmul(mix_c, w_out, F32, name="proj_out").reshape(b, n_ctx, d)
        ctx, hf_ctx = norm_step(ctx, resid=(y_ctx, m_ctx[2], p["norm_post_mix"]),
                                norm=(p["norm_pre_ffn"], m_ctx[3], m_ctx[4]))

    w_gate = p["ffn_w_gate"].astype(BF16)
    w_up = p["ffn_w_up"].astype(BF16)
    w_down = p["ffn_w_down"].astype(BF16)

    def ffn(h, n_tok):
        h2 = h.reshape(b * n_tok, d)
        g = matmul(h2, w_gate, BF16, name="ffn_gate").reshape(b, n_tok, -1)
        u = matmul(h2, w_up, BF16, name="ffn_up").reshape(b, n_tok, -1)
        act = ffn_act(g, u, p["ffn_conv"]).reshape(b * n_tok, -1)
        return matmul(act, w_down, F32, name="ffn_down").reshape(b, n_tok, d)

    f_lat = ffn(hf_lat, s)
    x, h_lat = norm_step(x, resid=(f_lat, m_lat[5], p["norm_post_ffn"]),
                         norm=None if next_norm is None else next_norm[0])
    h_ctx = None
    if with_ctx:
        f_ctx = ffn(hf_ctx, n_ctx)
        ctx, h_ctx = norm_step(ctx, resid=(f_ctx, m_ctx[5], p["norm_post_ffn"]),
                               norm=None if next_norm is None else next_norm[1])
    return x, ctx, h_lat, h_ctx


def kernel(x, c, ctx, c_ctx, w_mod, b_mod, norm_pre_mix, norm_post_mix, norm_pre_ffn, norm_post_ffn,
           w_in, gdn_conv, gdn_a_log, gdn_dt_bias, gdn_norm, swa_sink, diff_lambda, diff_norm, w_out,
           ffn_w_gate, ffn_w_up, ffn_conv, ffn_w_down):
    b, s, d = x.shape
    depth = w_mod.shape[0]
    rope_tabs = (_rope_tables(s, HEAD_DIM), _rope_tables(s, DIFF_QK_DIM))

    cc = jnp.zeros((8, d), F32).at[:b].set(c).at[b].set(c_ctx)
    mods = []
    for layer in range(depth):
        m = modulation(cc, w_mod[layer], b_mod[layer])
        m_lat = [m[:b, None, j * d:(j + 1) * d] for j in range(6)]
        m_ctx = [jnp.broadcast_to(m[b:b + 1, None, j * d:(j + 1) * d], (b, 1, d)) for j in range(6)]
        mods.append((m_lat, m_ctx))

    def pre_mix(layer):
        m_lat, m_ctx = mods[layer]
        return ((norm_pre_mix[layer], m_lat[0], m_lat[1]), (norm_pre_mix[layer], m_ctx[0], m_ctx[1]))

    n0 = pre_mix(0)
    _, h_lat = norm_step(x, norm=n0[0])
    _, h_ctx = norm_step(ctx, norm=n0[1])
    for layer in range(depth):
        with_ctx = layer < depth - 1
        p = dict(w_in=w_in[layer], gdn_conv=gdn_conv[layer], gdn_a_log=gdn_a_log[layer],
                 gdn_dt_bias=gdn_dt_bias[layer], gdn_norm=gdn_norm[layer], swa_sink=swa_sink[layer],
                 diff_lambda=diff_lambda[layer], diff_norm=diff_norm[layer], w_out=w_out[layer],
                 norm_post_mix=norm_post_mix[layer], norm_pre_ffn=norm_pre_ffn[layer],
                 norm_post_ffn=norm_post_ffn[layer], ffn_w_gate=ffn_w_gate[layer],
                 ffn_w_up=ffn_w_up[layer], ffn_conv=ffn_conv[layer], ffn_w_down=ffn_w_down[layer])
        m_lat, m_ctx = mods[layer]
        next_norm = pre_mix(layer + 1) if layer + 1 < depth else None
        x, ctx, h_lat, h_ctx = _layer(x, ctx, h_lat, h_ctx, m_lat, m_ctx, p, layer, with_ctx,
                                      rope_tabs, next_norm)
    return x
```

```python
import functools
import math

import jax
import jax.numpy as jnp
import numpy as np
from jax import lax
from jax.experimental import pallas as pl
from jax.experimental.pallas import tpu as pltpu

F32 = jnp.float32
BF16 = jnp.bfloat16

GRID_W = 64
HEAD_DIM = 128
LANES = 128
EPS = 1e-6
ROPE_BASE = 10000.0

GDN_HEADS = 6
GDN_CONV = 5
GDN_CHUNK = 64
GDN_W = GDN_HEADS * HEAD_DIM
SWA_HEADS = 6
SWA_KV_HEADS = 2
SWA_GROUP = 3
SWA_WINDOW = 128
SWA_BLOCK = 128
DIFF_HEADS = 4
DIFF_QK_DIM = 64
FFN_CONV = 3

VMEM_LIMIT_BYTES = 56 * 1024 * 1024


def _params(*sem):
    return pltpu.CompilerParams(dimension_semantics=sem, vmem_limit_bytes=VMEM_LIMIT_BYTES)


def _silu(x):
    return x * jax.nn.sigmoid(x)


def _mm_kernel(a_ref, b_ref, o_ref):
    o_ref[...] = jnp.dot(a_ref[...], b_ref[...], preferred_element_type=F32).astype(o_ref.dtype)


def matmul(a, b, out_dtype, tm=2048, tn=512, name="matmul"):
    m, k = a.shape
    n = b.shape[1]
    tm = min(tm, m)
    tn = min(tn, n)
    assert m % tm == 0 and n % tn == 0, (m, n, tm, tn)
    return pl.pallas_call(
        _mm_kernel,
        grid=(m // tm, n // tn),
        in_specs=[pl.BlockSpec((tm, k), lambda i, j: (i, 0)),
                  pl.BlockSpec((k, tn), lambda i, j: (0, j))],
        out_specs=pl.BlockSpec((tm, tn), lambda i, j: (i, j)),
        out_shape=jax.ShapeDtypeStruct((m, n), out_dtype),
        compiler_params=_params("parallel", "parallel"),
        name=name,
    )(a, b)


def _mod_kernel(c_ref, w_ref, b_ref, o_ref):
    x = _silu(c_ref[...]).astype(BF16)
    o_ref[...] = jnp.dot(x, w_ref[...].astype(BF16), preferred_element_type=F32) + b_ref[...]


def modulation(cc, w, b):
    m, k = cc.shape
    n = w.shape[1]
    tn = 1024
    return pl.pallas_call(
        _mod_kernel,
        grid=(n // tn,),
        in_specs=[pl.BlockSpec((m, k), lambda j: (0, 0)),
                  pl.BlockSpec((k, tn), lambda j: (0, j)),
                  pl.BlockSpec((1, tn), lambda j: (0, j))],
        out_specs=pl.BlockSpec((m, tn), lambda j: (0, j)),
        out_shape=jax.ShapeDtypeStruct((m, n), F32),
        compiler_params=_params("parallel"),
        name="modulation",
    )(cc, w, b.reshape(1, n))


def _rms(x, g):
    return x * lax.rsqrt(jnp.mean(x * x, axis=-1, keepdims=True) + EPS) * g


def _norm_kernel(has_resid, has_norm, *refs):
    refs = list(refs)
    x = refs.pop(0)[0]
    if has_resid:
        y = refs.pop(0)[0]
        gate = refs.pop(0)[0]
        gpost = refs.pop(0)[...]
        x = x + gate * _rms(y, gpost)
    if has_norm:
        gpre = refs.pop(0)[...]
        shift = refs.pop(0)[0]
        scale = refs.pop(0)[0]
    if has_resid:
        refs.pop(0)[0] = x
    if has_norm:
        refs.pop(0)[0] = (_rms(x, gpre) * (1.0 + scale) + shift).astype(BF16)


def norm_step(x, resid=None, norm=None, ts=512):
    b, s, d = x.shape
    ts = min(ts, s)
    assert s % ts == 0
    tok = pl.BlockSpec((1, ts, d), lambda i, j: (i, j, 0))
    vec = pl.BlockSpec((1, 1, d), lambda i, j: (i, 0, 0))
    gsp = pl.BlockSpec((1, d), lambda i, j: (0, 0))
    args, specs, outs, out_specs = [x], [tok], [], []
    if resid is not None:
        y, gate, gpost = resid
        args += [y, gate, gpost.reshape(1, d)]
        specs += [tok, vec, gsp]
        outs.append(jax.ShapeDtypeStruct((b, s, d), F32))
        out_specs.append(tok)
    if norm is not None:
        gpre, shift, scale = norm
        args += [gpre.reshape(1, d), shift, scale]
        specs += [gsp, vec, vec]
        outs.append(jax.ShapeDtypeStruct((b, s, d), BF16))
        out_specs.append(tok)
    res = pl.pallas_call(
        functools.partial(_norm_kernel, resid is not None, norm is not None),
        grid=(b, s // ts),
        in_specs=specs, out_specs=out_specs, out_shape=outs,
        compiler_params=_params("parallel", "parallel"),
        name="norm_step",
    )(*args)
    res = list(res)
    x_new = res.pop(0) if resid is not None else None
    h = res.pop(0) if norm is not None else None
    return x_new, h


def _seq_conv(x, prev8, next8, w, halo):
    t = x.shape[0]
    xe = jnp.concatenate([prev8, x, next8], axis=0)
    acc = None
    for j in range(2 * halo + 1):
        off = 8 + j - halo
        term = xe[off:off + t] * w[j:j + 1]
        acc = term if acc is None else acc + term
    return acc


def _halo_specs(ts, rows, width, col_of):
    per = ts // rows

    def main(b, i, c):
        return (b, i, col_of(c))

    def prev(b, i, c):
        return (b, jnp.maximum(i * per - 1, 0), col_of(c))

    def nxt(nblk):
        def f(b, i, c):
            return (b, jnp.minimum((i + 1) * per, nblk * per - 1), col_of(c))
        return f

    return main, prev, nxt


def _ffn_act_kernel(g_ref, gp_ref, gn_ref, u_ref, w_ref, o_ref):
    i = pl.program_id(1)
    last = pl.num_programs(1) - 1
    g = g_ref[0].astype(F32)
    prev8 = gp_ref[0].astype(F32)[8:16] * (i > 0).astype(F32)
    next8 = gn_ref[0].astype(F32)[0:8] * (i < last).astype(F32)
    y = _seq_conv(g, prev8, next8, w_ref[...], FFN_CONV // 2)
    o_ref[0] = (_silu(y) * u_ref[0].astype(F32)).astype(BF16)


def ffn_act(g, u, conv_w, ts=512, tc=512):
    b, s, f = g.shape
    ts = min(ts, s)
    nblk = s // ts
    main, prev, nxt = _halo_specs(ts, 16, tc, lambda c: c)
    wpad = jnp.zeros((8, f), F32).at[:FFN_CONV].set(conv_w)
    return pl.pallas_call(
        _ffn_act_kernel,
        grid=(b, nblk, f // tc),
        in_specs=[pl.BlockSpec((1, ts, tc), main),
                  pl.BlockSpec((1, 16, tc), prev),
                  pl.BlockSpec((1, 16, tc), nxt(nblk)),
                  pl.BlockSpec((1, ts, tc), main),
                  pl.BlockSpec((8, tc), lambda bb, i, c: (0, c))],
        out_specs=pl.BlockSpec((1, ts, tc), main),
        out_shape=jax.ShapeDtypeStruct((b, s, f), BF16),
        compiler_params=_params("parallel", "parallel", "parallel"),
        name="ffn_act",
    )(g, g, g, u, wpad)


def _rope_tables(n_tokens, dim):
    quarter = dim // 4
    inv_freq = ROPE_BASE ** (-jnp.arange(quarter, dtype=F32) / quarter)
    tok = jnp.arange(n_tokens)
    rows = (tok // GRID_W).astype(F32)
    cols = (tok % GRID_W).astype(F32)
    ang_r = rows[:, None] * inv_freq
    ang_c = cols[:, None] * inv_freq
    ang = jnp.concatenate([ang_r, ang_r, ang_c, ang_c], axis=-1)
    ang = jnp.tile(ang, (1, LANES // dim))
    cos, sin = jnp.cos(ang), jnp.sin(ang)
    first = (jnp.arange(LANES) % (2 * quarter)) < quarter
    sin_a = jnp.where(first, -sin, 0.0)
    sin_b = jnp.where(first, 0.0, sin)
    return jnp.stack([cos, sin_a, sin_b])


def _rope_kernel(x_ref, tb_ref, tc_ref, o_ref):
    j = pl.program_id(2)
    ngroups = x_ref.shape[2] // LANES

    def apply(tab_ref, quarter):
        cos, sin_a, sin_b = tab_ref[0], tab_ref[1], tab_ref[2]
        for gidx in range(ngroups):
            sl = slice(gidx * LANES, (gidx + 1) * LANES)
            x = x_ref[0, :, sl]
            up = pltpu.roll(x, LANES - quarter, axis=1)
            dn = pltpu.roll(x, quarter, axis=1)
            o_ref[0, :, sl] = (x * cos + up * sin_a + dn * sin_b).astype(BF16)

    @pl.when(j == 0)
    def _():
        apply(tb_ref, HEAD_DIM // 4)

    @pl.when(j == 1)
    def _():
        apply(tc_ref, DIFF_QK_DIM // 4)


def rope_qk(x, tab_b, tab_c, ts=512):
    b, s, w = x.shape
    half = w // 2
    ts = min(ts, s)
    return pl.pallas_call(
        _rope_kernel,
        grid=(b, s // ts, 2),
        in_specs=[pl.BlockSpec((1, ts, half), lambda bb, i, j: (bb, i, j)),
                  pl.BlockSpec((3, ts, LANES), lambda bb, i, j: (0, i, 0)),
                  pl.BlockSpec((3, ts, LANES), lambda bb, i, j: (0, i, 0))],
        out_specs=pl.BlockSpec((1, ts, half), lambda bb, i, j: (bb, i, j)),
        out_shape=jax.ShapeDtypeStruct((b, s, w), BF16),
        compiler_params=_params("parallel", "parallel", "parallel"),
        name="rope_qk",
    )(x, tab_b, tab_c)


def _gdn_feat_kernel(x_ref, xp_ref, xn_ref, w_ref, o_ref):
    i = pl.program_id(1)
    c = pl.program_id(2)
    last = pl.num_programs(1) - 1
    prev8 = xp_ref[0] * (i > 0).astype(F32)
    next8 = xn_ref[0] * (i < last).astype(F32)
    y = _silu(_seq_conv(x_ref[0], prev8, next8, w_ref[...], GDN_CONV // 2))
    inv = lax.rsqrt(jnp.sum(y * y, axis=-1, keepdims=True) + EPS)
    kind = c // GDN_HEADS
    inv = jnp.where(kind == 0, inv * (HEAD_DIM ** -0.5), inv)
    inv = jnp.where(kind == 2, jnp.ones_like(inv), inv)
    o_ref[0] = y * inv


def gdn_features(proj, conv_w, ts=512):
    b, l, _ = proj.shape
    ts = min(ts, l)
    nblk = l // ts
    ncol = 3 * GDN_HEADS
    main, prev, nxt = _halo_specs(ts, 8, LANES, lambda c: c)
    wpad = jnp.zeros((8, ncol * LANES), F32).at[:GDN_CONV].set(conv_w)
    return pl.pallas_call(
        _gdn_feat_kernel,
        grid=(b, nblk, ncol),
        in_specs=[pl.BlockSpec((1, ts, LANES), main),
                  pl.BlockSpec((1, 8, LANES), prev),
                  pl.BlockSpec((1, 8, LANES), nxt(nblk)),
                  pl.BlockSpec((8, LANES), lambda bb, i, c: (0, c))],
        out_specs=pl.BlockSpec((1, ts, LANES), main),
        out_shape=jax.ShapeDtypeStruct((b, l, ncol * LANES), F32),
        compiler_params=_params("parallel", "parallel", "parallel"),
        name="gdn_features",
    )(proj, proj, proj, wpad)


def _split3(x):
    hi = x.astype(BF16)
    r = x - hi.astype(F32)
    mid = r.astype(BF16)
    lo = (r - mid.astype(F32)).astype(BF16)
    return hi, mid, lo


def _dot(a, b):
    return jnp.dot(a, b, preferred_element_type=F32)


def _mask_dot(mask_bf16, x):
    hi, mid, lo = _split3(x)
    return _dot(mask_bf16, hi) + _dot(mask_bf16, mid) + _dot(mask_bf16, lo)


def _split2(x):
    hi = x.astype(BF16)
    lo = (x - hi.astype(F32)).astype(BF16)
    return hi, lo


def _dot3(a, b):
    return _dot(a[0], b[0]) + _dot(a[0], b[1]) + _dot(a[1], b[0])


def _unit_tri_inverse(a_list, same16, same32):
    n = a_list[0].shape[0]
    eye = (lax.broadcasted_iota(jnp.int32, (n, n), 0) == lax.broadcasted_iota(jnp.int32, (n, n), 1)).astype(F32)
    off16 = jnp.logical_and(same32, jnp.logical_not(same16))
    d = [jnp.where(same16, a, 0.0) for a in a_list]
    l1s = [_split2(jnp.where(off16, a, 0.0)) for a in a_list]
    l2s = [_split2(jnp.where(same32, 0.0, a)) for a in a_list]
    ds = [_split2(x) for x in d]
    d2 = [_dot3(x, x) for x in ds]
    d2s = [_split2(x) for x in d2]
    d3 = [_dot3(x2, x) for x2, x in zip(d2s, ds)]
    d4s = [_split2(_dot3(x2, x2)) for x2 in d2s]
    p = [eye - x + x2 - x3 for x, x2, x3 in zip(d, d2, d3)]
    ps = [_split2(x) for x in p]
    p = [x + _dot3(xs, x4) for x, xs, x4 in zip(p, ps, d4s)]
    d8s = [_split2(_dot3(x4, x4)) for x4 in d4s]
    ps = [_split2(x) for x in p]
    t = [x + _dot3(xs, x8) for x, xs, x8 in zip(p, ps, d8s)]
    for ls in (l1s, l2s):
        ts = [_split2(x) for x in t]
        xs = [_split2(_dot3(l, y)) for l, y in zip(ls, ts)]
        t = [x - _dot3(y, z) for x, y, z in zip(t, ts, xs)]
    return [_split2(x) for x in t]


def _gdn_kernel(reverse, q_ref, k_ref, v_ref, gt_ref, cst_ref, s0_ref, o_ref, sfin_ref,
                s_scr, g_scr, cum_scr, tot_scr, xt_scr):
    i = pl.program_id(1)
    nblk = pl.num_programs(1)
    tb = q_ref.shape[1]
    npairs = tb // 128
    d = 1 if reverse else 0
    heads = range(GDN_HEADS)

    @pl.when(i == 0)
    def _():
        s_scr[...] = s0_ref[0]

    raw = gt_ref[0]
    lane = lax.broadcasted_iota(jnp.int32, raw.shape, 1)
    neg_a = cst_ref[0:1, :]
    dt_bias = cst_ref[1:2, :]
    gates = jnp.where(lane < 2 * GDN_HEADS, jax.nn.sigmoid(raw), neg_a * jax.nn.softplus(raw + dt_bias))
    g_scr[...] = gates

    r64 = lax.broadcasted_iota(jnp.int32, (GDN_CHUNK, GDN_CHUNK), 0)
    c64 = lax.broadcasted_iota(jnp.int32, (GDN_CHUNK, GDN_CHUNK), 1)
    incl = (r64 <= c64) if reverse else (r64 >= c64)
    strict = (r64 < c64) if reverse else (r64 > c64)
    same16 = (r64 // 16) == (c64 // 16)
    same32 = (r64 // 32) == (c64 // 32)

    r128 = lax.broadcasted_iota(jnp.int32, (128, 128), 0)
    c128 = lax.broadcasted_iota(jnp.int32, (128, 128), 1)
    same_chunk = (r128 // GDN_CHUNK) == (c128 // GDN_CHUNK)
    order = (r128 <= c128) if reverse else (r128 >= c128)
    cum_mask = jnp.logical_and(same_chunk, order).astype(BF16)
    tot_mask = same_chunk.astype(BF16)
    for p in range(npairs):
        rows = slice(p * 128, (p + 1) * 128)
        gp = gates[rows]
        cum = _mask_dot(cum_mask, gp)
        tot = _mask_dot(tot_mask, gp)
        cum_scr[rows, :] = cum
        tot_scr[rows, :] = tot
        lane_p = lax.broadcasted_iota(jnp.int32, gp.shape, 1)
        xt_scr[p] = jnp.where(lane_p < 2 * GDN_HEADS, gp, cum).T

    chunk_order = (1, 0) if reverse else (0, 1)

    def pair_body(pp, carry):
        p = (npairs - 1 - pp) if reverse else pp
        row0 = pl.multiple_of(p * 128, 128)
        gp = g_scr[pl.ds(row0, 128), :]
        cump = cum_scr[pl.ds(row0, 128), :]
        totp = tot_scr[pl.ds(row0, 128), :]
        xtp = xt_scr[p]
        ecum = jnp.exp(cump)

        units = [(h, cc) for cc in chunk_order for h in heads]
        hsl = [slice(h * HEAD_DIM, (h + 1) * HEAD_DIM) for h in heads]
        qp = [q_ref[0, pl.ds(row0, 128), hsl[h]] for h in heads]
        kp = [k_ref[0, pl.ds(row0, 128), hsl[h]] for h in heads]
        vp = [v_ref[0, pl.ds(row0, 128), hsl[h]] for h in heads]
        ktp = [x.T for x in kp]

        def rs(cc):
            return slice(cc * GDN_CHUNK, (cc + 1) * GDN_CHUNK)

        def col(arr, cc, j):
            return arr[rs(cc), j:j + 1]

        bi = [d * GDN_HEADS + h for h in heads]
        gi = [2 * GDN_HEADS + d * GDN_HEADS + h for h in heads]
        bcol = [col(gp, cc, bi[h]) for h, cc in units]
        ecol = [col(ecum, cc, gi[h]) for h, cc in units]
        crow = [xtp[gi[h]:gi[h] + 1, rs(cc)] for h, cc in units]
        tot11 = [totp[cc * GDN_CHUNK:cc * GDN_CHUNK + 1, gi[h]:gi[h] + 1] for h, cc in units]
        decay = [jnp.exp(jnp.where(incl, col(cump, cc, gi[h]) - cr, -jnp.inf))
                 for (h, cc), cr in zip(units, crow)]
        kh = [kp[h][rs(cc)] for h, cc in units]
        qh = [qp[h][rs(cc)] for h, cc in units]
        vh = [vp[h][rs(cc)] for h, cc in units]
        kt = [ktp[h][:, rs(cc)] for h, cc in units]
        ktb = [x.astype(BF16) for x in kt]
        kk = [_dot(x.astype(BF16), y) for x, y in zip(kh, ktb)]
        qk = [_dot(x.astype(BF16), y) * dc for x, y, dc in zip(qh, ktb, decay)]
        a = [jnp.where(strict, b_ * x * dc, 0.0) for b_, x, dc in zip(bcol, kk, decay)]
        ts = _unit_tri_inverse(a, same16, same32)
        rhs = [_split2(jnp.concatenate([b_ * v_, (b_ * e_) * k_], axis=1))
               for b_, e_, v_, k_ in zip(bcol, ecol, vh, kh)]
        uw = [_dot3(t_, r_) for t_, r_ in zip(ts, rhs)]
        lhs_s = [jnp.concatenate([x[:, HEAD_DIM:], q_ * e_], axis=0).astype(BF16)
                 for x, q_, e_ in zip(uw, qh, ecol)]
        lhs_v = [jnp.concatenate([x, y * jnp.exp(t11 - cr)], axis=0).astype(BF16)
                 for x, y, t11, cr in zip(qk, kt, tot11, crow)]
        dec = [jnp.exp(x) for x in tot11]

        s = [s_scr[h] for h in heads]
        outs = {}
        for ci, cc in enumerate(chunk_order):
            un = [ci * GDN_HEADS + h for h in heads]
            sb = [x.astype(BF16) for x in s]
            rs_ = [_dot(lhs_s[j], sb[h]) for h, j in zip(heads, un)]
            vnb = [(uw[j][:, :HEAD_DIM] - r_[:GDN_CHUNK]).astype(BF16) for j, r_ in zip(un, rs_)]
            rv = [_dot(lhs_v[j], x) for j, x in zip(un, vnb)]
            for h, j in zip(heads, un):
                outs[(h, cc)] = rs_[h][GDN_CHUNK:] + rv[h][:GDN_CHUNK]
            s = [x * dec[j] + r_[GDN_CHUNK:] for x, j, r_ in zip(s, un, rv)]
        for h in heads:
            s_scr[h] = s[h]
            o_ref[0, pl.ds(row0, 128), hsl[h]] = jnp.concatenate([outs[(h, 0)], outs[(h, 1)]], axis=0)
        return carry

    lax.fori_loop(0, npairs, pair_body, 0)

    @pl.when(i == nblk - 1)
    def _():
        sfin_ref[0] = s_scr[...]


def gdn_scan(feat, gates, cst, s0, reverse, tb=512):
    b, l, _ = feat.shape
    tb = min(tb, l)
    nblk = l // tb
    gate_col = gates.shape[2] // LANES - 1

    def blk(i):
        return (nblk - 1 - i) if reverse else i

    def tok(col):
        return lambda bb, i: (bb, blk(i), col)

    st_spec = pl.BlockSpec((1, GDN_HEADS, HEAD_DIM, HEAD_DIM), lambda bb, i: (bb, 0, 0, 0))
    return pl.pallas_call(
        functools.partial(_gdn_kernel, reverse),
        grid=(b, nblk),
        in_specs=[pl.BlockSpec((1, tb, GDN_W), tok(0)),
                  pl.BlockSpec((1, tb, GDN_W), tok(1)),
                  pl.BlockSpec((1, tb, GDN_W), tok(2)),
                  pl.BlockSpec((1, tb, LANES), tok(gate_col)),
                  pl.BlockSpec((8, LANES), lambda bb, i: (0, 0)),
                  st_spec],
        out_specs=[pl.BlockSpec((1, tb, GDN_W), tok(0)), st_spec],
        out_shape=[jax.ShapeDtypeStruct((b, l, GDN_W), F32),
                   jax.ShapeDtypeStruct((b, GDN_HEADS, HEAD_DIM, HEAD_DIM), F32)],
        scratch_shapes=[pltpu.VMEM((GDN_HEADS, HEAD_DIM, HEAD_DIM), F32),
                        pltpu.VMEM((tb, LANES), F32),
                        pltpu.VMEM((tb, LANES), F32),
                        pltpu.VMEM((tb, LANES), F32),
                        pltpu.VMEM((tb // 128, LANES, 128), F32)],
        compiler_params=_params("parallel", "arbitrary"),
        name="gdn_scan_bwd" if reverse else "gdn_scan_fwd",
    )(feat, feat, feat, gates, cst, s0)


def _gdn_out_kernel(of_ref, ob_ref, z_ref, g_ref, o_ref):
    o = of_ref[0] + ob_ref[0]
    o_ref[0] = (_rms(o, g_ref[...]) * _silu(z_ref[0])).astype(BF16)


def gdn_output(o_f, o_b, proj, norm_w, ts=512):
    b, l, w = o_f.shape
    ts = min(ts, l)
    z_col0 = 3 * GDN_HEADS
    tok = lambda bb, i, h: (bb, i, h)
    return pl.pallas_call(
        _gdn_out_kernel,
        grid=(b, l // ts, GDN_HEADS),
        in_specs=[pl.BlockSpec((1, ts, HEAD_DIM), tok),
                  pl.BlockSpec((1, ts, HEAD_DIM), tok),
                  pl.BlockSpec((1, ts, HEAD_DIM), lambda bb, i, h: (bb, i, z_col0 + h)),
                  pl.BlockSpec((1, HEAD_DIM), lambda bb, i, h: (0, 0))],
        out_specs=pl.BlockSpec((1, ts, HEAD_DIM), tok),
        out_shape=jax.ShapeDtypeStruct((b, l, w), BF16),
        compiler_params=_params("parallel", "parallel", "parallel"),
        name="gdn_output",
    )(o_f, o_b, proj, norm_w.reshape(1, HEAD_DIM))


def _dot_nt(a, b):
    return lax.dot_general(a, b, (((1,), (1,)), ((), ())), preferred_element_type=F32)


def _swa_kernel(windowed, sink_ref, q_ref, *refs):
    if windowed:
        kp_ref, kc_ref, kn_ref, vp_ref, vc_ref, vn_ref, kx_ref, vx_ref, o_ref = refs
    else:
        kx_ref, vx_ref, o_ref = refs
    hk = pl.program_id(1)
    i = pl.program_id(2)
    nb = pl.num_programs(2)
    scale = HEAD_DIM ** -0.5
    kx = kx_ref[0]
    vx = vx_ref[0]
    if windowed:
        kw = jnp.concatenate([kp_ref[0], kc_ref[0], kn_ref[0]], axis=0)
        vw = jnp.concatenate([vp_ref[0], vc_ref[0], vn_ref[0]], axis=0)
        r = lax.broadcasted_iota(jnp.int32, (SWA_BLOCK, 3 * SWA_BLOCK), 0)
        c = lax.broadcasted_iota(jnp.int32, (SWA_BLOCK, 3 * SWA_BLOCK), 1)
        mask = jnp.abs(c - SWA_BLOCK - r) <= SWA_WINDOW
        mask = jnp.logical_and(mask, jnp.logical_or(i > 0, c >= SWA_BLOCK))
        mask = jnp.logical_and(mask, jnp.logical_or(i < nb - 1, c < 2 * SWA_BLOCK))
    for g in range(SWA_GROUP):
        q = q_ref[0, :, g * HEAD_DIM:(g + 1) * HEAD_DIM]
        sink = sink_ref[hk * SWA_GROUP + g]
        s_ctx = _dot_nt(q, kx) * scale
        m = jnp.maximum(jnp.max(s_ctx, axis=-1, keepdims=True), sink)
        if windowed:
            s_win = jnp.where(mask, _dot_nt(q, kw) * scale, -jnp.inf)
            m = jnp.maximum(m, jnp.max(s_win, axis=-1, keepdims=True))
        e_ctx = jnp.exp(s_ctx - m)
        den = jnp.sum(e_ctx, axis=-1, keepdims=True) + jnp.exp(sink - m)
        if windowed:
            e_win = jnp.exp(s_win - m)
            den = den + jnp.sum(e_win, axis=-1, keepdims=True)
        inv = 1.0 / den
        o = _dot((e_ctx * inv).astype(BF16), vx)
        if windowed:
            o = o + _dot((e_win * inv).astype(BF16), vw)
        o_ref[0, :, g * HEAD_DIM:(g + 1) * HEAD_DIM] = o.astype(BF16)


def swa_attention(q, k, v, k_ctx, v_ctx, sink, windowed):
    b, s, _ = q.shape
    nb = s // SWA_BLOCK
    if windowed:
        (k_arr, k_col), (v_arr, v_col) = k, v
    (kx_arr, kx_col), (vx_arr, vx_col) = k_ctx, v_ctx
    n_ctx = kx_arr.shape[1]
    qspec = pl.BlockSpec((1, SWA_BLOCK, SWA_GROUP * HEAD_DIM), lambda bb, hk, i: (bb, i, hk))

    def band(col, shift):
        return pl.BlockSpec((1, SWA_BLOCK, HEAD_DIM),
                            lambda bb, hk, i: (bb, jnp.clip(i + shift, 0, nb - 1), col + hk))

    def ctx(col):
        return pl.BlockSpec((1, n_ctx, HEAD_DIM), lambda bb, hk, i: (bb, 0, col + hk))

    specs = [pl.BlockSpec(memory_space=pltpu.SMEM), qspec]
    args = [sink, q]
    if windowed:
        specs += [band(k_col, -1), band(k_col, 0), band(k_col, 1),
                  band(v_col, -1), band(v_col, 0), band(v_col, 1)]
        args += [k_arr] * 3 + [v_arr] * 3
    specs += [ctx(kx_col), ctx(vx_col)]
    args += [kx_arr, vx_arr]
    return pl.pallas_call(
        functools.partial(_swa_kernel, windowed),
        grid=(b, SWA_KV_HEADS, nb),
        in_specs=specs,
        out_specs=qspec,
        out_shape=jax.ShapeDtypeStruct((b, s, SWA_HEADS * HEAD_DIM), BF16),
        compiler_params=_params("parallel", "parallel", "parallel"),
        name="swa_attention",
    )(*args)


def _diff_kernel(kc, lam_init, lam_ref, nrm_ref, q_ref, k_ref, v_ref, o_ref, sa_ref, sb_ref):
    tq = q_ref.shape[1]
    nk = k_ref.shape[1] // kc
    q = q_ref[0]
    lane = lax.broadcasted_iota(jnp.int32, q.shape, 1)
    zero = jnp.zeros_like(q)
    qmaps = (jnp.where(lane < DIFF_QK_DIM, q, zero), jnp.where(lane >= DIFF_QK_DIM, q, zero))
    ones_col = (lax.broadcasted_iota(jnp.int32, (kc, HEAD_DIM), 1) == 0).astype(BF16)

    def logits(j, s_ref):
        kblk = k_ref[0, pl.ds(pl.multiple_of(j * kc, kc), kc), :]
        for mi in range(2):
            s_ref[mi] = _dot_nt(qmaps[mi], kblk).astype(BF16)

    def softmax_pv(j, s_ref, carry):
        row0 = pl.multiple_of(j * kc, kc)
        out = []
        for mi in range(2):
            m, acc = carry[2 * mi:2 * mi + 2]
            m_new = jnp.maximum(m, jnp.max(s_ref[mi], axis=-1, keepdims=True).astype(F32))
            acc = jnp.exp(m - m_new) * acc
            e = jnp.exp(s_ref[mi] - m_new.astype(BF16))
            v1 = jnp.concatenate([v_ref[0, pl.ds(row0, kc), :], ones_col], axis=1)
            acc = acc + _dot(e, v1)
            out += [m_new, acc]
        return tuple(out)

    logits(0, sa_ref)

    def body(t, carry):
        logits(2 * t + 1, sb_ref)
        carry = softmax_pv(2 * t, sa_ref, carry)
        logits(jnp.minimum(2 * t + 2, nk - 1), sa_ref)
        return softmax_pv(2 * t + 1, sb_ref, carry)

    neg = jnp.full((tq, 1), -jnp.inf, F32)
    zacc = jnp.zeros((tq, 2 * HEAD_DIM), F32)
    carry = lax.fori_loop(0, nk // 2, body, (neg, zacc, neg, zacc))
    if nk % 2:
        carry = softmax_pv(nk - 1, sa_ref, carry)
    _, a0, _, a1 = carry
    lam = lam_ref[...]
    lam_full = (jnp.exp(jnp.sum(lam[0:1] * lam[1:2], axis=-1, keepdims=True))
                - jnp.exp(jnp.sum(lam[2:3] * lam[3:4], axis=-1, keepdims=True)) + lam_init)
    o = (a0[:, :HEAD_DIM] / a0[:, HEAD_DIM:HEAD_DIM + 1]
         - lam_full * (a1[:, :HEAD_DIM] / a1[:, HEAD_DIM:HEAD_DIM + 1]))
    o_ref[0] = (_rms(o, nrm_ref[...]) * (1.0 - lam_init)).astype(BF16)


def diff_attention(q, k, v, lam, norm_w, lam_init, tq=256):
    (q_arr, q_col), (k_arr, k_col), (v_arr, v_col) = q, k, v
    b, s, _ = q_arr.shape
    sk = k_arr.shape[1]
    tq = min(tq, s)
    kc = 1280 if sk % 1280 == 0 else 256
    assert s % tq == 0 and sk % kc == 0, (s, sk, tq, kc)
    return pl.pallas_call(
        functools.partial(_diff_kernel, kc, lam_init),
        grid=(b, DIFF_HEADS, s // tq),
        in_specs=[pl.BlockSpec((4, DIFF_QK_DIM), lambda bb, h, i: (0, 0)),
                  pl.BlockSpec((1, HEAD_DIM), lambda bb, h, i: (0, 0)),
                  pl.BlockSpec((1, tq, HEAD_DIM), lambda bb, h, i: (bb, i, q_col + h)),
                  pl.BlockSpec((1, sk, HEAD_DIM), lambda bb, h, i: (bb, 0, k_col + h)),
                  pl.BlockSpec((1, sk, HEAD_DIM), lambda bb, h, i: (bb, 0, v_col + h))],
        out_specs=pl.BlockSpec((1, tq, HEAD_DIM), lambda bb, h, i: (bb, i, h)),
        out_shape=jax.ShapeDtypeStruct((b, s, DIFF_HEADS * HEAD_DIM), BF16),
        scratch_shapes=[pltpu.VMEM((2, tq, kc), BF16), pltpu.VMEM((2, tq, kc), BF16)],
        compiler_params=_params("parallel", "parallel", "parallel"),
        name="diff_attention",
    )(lam, norm_w.reshape(1, HEAD_DIM), q_arr, k_arr, v_arr)


_O_QKV, _O_Z, _O_BETA, _O_ALPHA, _O_SQ, _O_SK, _O_SV, _O_DQ, _O_DK, _O_DV, _O_END = (
    0, 2304, 3072, 3084, 3096, 3864, 4120, 4376, 4888, 5400, 5912)


def _split_w_in(w):
    gates = jnp.zeros((w.shape[0], LANES), w.dtype).at[:, :_O_SQ - _O_BETA].set(w[:, _O_BETA:_O_SQ])
    w_gdn = jnp.concatenate([w[:, _O_QKV:_O_BETA], gates], axis=1)
    dq = w[:, _O_DQ:_O_DK] * (DIFF_QK_DIM ** -0.5)
    w_qk = jnp.concatenate([w[:, _O_SQ:_O_SV], dq, w[:, _O_DK:_O_DV]], axis=1)
    w_v = jnp.concatenate([w[:, _O_SV:_O_DQ], w[:, _O_DV:_O_END]], axis=1)
    return w_gdn.astype(BF16), w_qk.astype(BF16), w_v.astype(BF16)


def _project(h, w_gdn, w_qk, w_v, qk_dtype):
    b, s, d = h.shape
    h2 = h.reshape(b * s, d)
    p_gdn = matmul(h2, w_gdn, F32, tn=640, name="proj_gdn").reshape(b, s, -1)
    p_qk = matmul(h2, w_qk, qk_dtype, name="proj_qk").reshape(b, s, -1)
    p_v = matmul(h2, w_v, BF16, tn=768, name="proj_v").reshape(b, s, -1)
    return p_gdn, p_qk, p_v


def _gdn_consts(a_log, dt_bias):
    cst = jnp.zeros((8, LANES), F32)
    cst = cst.at[0, 2 * GDN_HEADS:4 * GDN_HEADS].set(-jnp.exp(a_log.astype(F32)).reshape(-1))
    cst = cst.at[1, 2 * GDN_HEADS:4 * GDN_HEADS].set(dt_bias.astype(F32).reshape(-1))
    return cst


def _layer(x, ctx, h_lat, h_ctx, m_lat, m_ctx, p, layer, with_ctx, rope_tabs, next_norm):
    b, s, d = x.shape
    n_ctx = ctx.shape[1]
    w_gdn, w_qk, w_v = _split_w_in(p["w_in"])
    w_out = p["w_out"].astype(BF16)
    lg, lqk, lv = _project(h_lat, w_gdn, w_qk, w_v, F32)
    cg, cqk, cv = _project(h_ctx, w_gdn, w_qk, w_v, BF16)

    cst = _gdn_consts(p["gdn_a_log"], p["gdn_dt_bias"])
    f_lat = gdn_features(lg, p["gdn_conv"])
    f_ctx = gdn_features(cg, p["gdn_conv"])
    s0 = jnp.zeros((b, GDN_HEADS, HEAD_DIM, HEAD_DIM), F32)
    o_lat, o_ctx = [], []
    for reverse in (False, True):
        oc, s_ctx = gdn_scan(f_ctx, cg, cst, s0, reverse)
        ol, _ = gdn_scan(f_lat, lg, cst, s_ctx, reverse)
        o_lat.append(ol)
        o_ctx.append(oc)
    a_lat = gdn_output(o_lat[0], o_lat[1], lg, p["gdn_norm"])

    lqk = rope_qk(lqk, *rope_tabs)
    sink = p["swa_sink"].astype(F32)
    sq_c, sk_c, dq_c, dk_c = 0, SWA_HEADS, SWA_HEADS + SWA_KV_HEADS, SWA_HEADS + SWA_KV_HEADS + DIFF_HEADS
    sv_c, dv_c = 0, SWA_KV_HEADS
    b_lat = swa_attention(lqk, (lqk, sk_c), (lv, sv_c), (cqk, sk_c), (cv, sv_c), sink, True)
    lam_init = 0.8 - 0.6 * math.exp(-0.3 * layer)
    lam = p["diff_lambda"].astype(F32)
    dk_all = jnp.concatenate([lqk[:, :, dk_c * LANES:], cqk[:, :, dk_c * LANES:]], axis=1)
    dv_all = jnp.concatenate([lv[:, :, dv_c * LANES:], cv[:, :, dv_c * LANES:]], axis=1)
    c_lat = diff_attention((lqk, dq_c), (dk_all, 0), (dv_all, 0), lam, p["diff_norm"], lam_init)
    mix = jnp.concatenate([a_lat, b_lat, c_lat], axis=-1).reshape(b * s, d)
    y_lat = matmul(mix, w_out, F32, name="proj_out").reshape(b, s, d)

    x, hf_lat = norm_step(x, resid=(y_lat, m_lat[2], p["norm_post_mix"]),
                          norm=(p["norm_pre_ffn"], m_lat[3], m_lat[4]))
    if with_ctx:
        a_ctx = gdn_output(o_ctx[0], o_ctx[1], cg, p["gdn_norm"])
        b_ctx = swa_attention(cqk, None, None, (cqk, sk_c), (cv, sv_c), sink, False)
        c_ctx = diff_attention((cqk, dq_c), (cqk, dk_c), (cv, dv_c), lam, p["diff_norm"], lam_init)
        mix_c = jnp.concatenate([a_ctx, b_ctx, c_ctx], axis=-1).reshape(b * n_ctx, d)
        y_ctx = matmul(mix_c, w_out, F32, name="proj_out").reshape(b, n_ctx, d)
        ctx, hf_ctx = norm_step(ctx, resid=(y_ctx, m_ctx[2], p["norm_post_mix"]),
                                norm=(p["norm_pre_ffn"], m_ctx[3], m_ctx[4]))

    w_gate = p["ffn_w_gate"].astype(BF16)
    w_up = p["ffn_w_up"].astype(BF16)
    w_down = p["ffn_w_down"].astype(BF16)

    def ffn(h, n_tok):
        h2 = h.reshape(b * n_tok, d)
        g = matmul(h2, w_gate, BF16, name="ffn_gate").reshape(b, n_tok, -1)
        u = matmul(h2, w_up, BF16, name="ffn_up").reshape(b, n_tok, -1)
        act = ffn_act(g, u, p["ffn_conv"]).reshape(b * n_tok, -1)
        return matmul(act, w_down, F32, tm=1024, name="ffn_down").reshape(b, n_tok, d)

    f_lat = ffn(hf_lat, s)
    x, h_lat = norm_step(x, resid=(f_lat, m_lat[5], p["norm_post_ffn"]),
                         norm=None if next_norm is None else next_norm[0])
    h_ctx = None
    if with_ctx:
        f_ctx = ffn(hf_ctx, n_ctx)
        ctx, h_ctx = norm_step(ctx, resid=(f_ctx, m_ctx[5], p["norm_post_ffn"]),
                               norm=None if next_norm is None else next_norm[1])
    return x, ctx, h_lat, h_ctx


def kernel(x, c, ctx, c_ctx, w_mod, b_mod, norm_pre_mix, norm_post_mix, norm_pre_ffn, norm_post_ffn,
           w_in, gdn_conv, gdn_a_log, gdn_dt_bias, gdn_norm, swa_sink, diff_lambda, diff_norm, w_out,
           ffn_w_gate, ffn_w_up, ffn_conv, ffn_w_down):
    b, s, d = x.shape
    depth = w_mod.shape[0]
    rope_tabs = (_rope_tables(s, HEAD_DIM), _rope_tables(s, DIFF_QK_DIM))

    cc = jnp.zeros((8, d), F32).at[:b].set(c).at[b].set(c_ctx)
    mods = []
    for layer in range(depth):
        m = modulation(cc, w_mod[layer], b_mod[layer])
        m_lat = [m[:b, None, j * d:(j + 1) * d] for j in range(6)]
        m_ctx = [jnp.broadcast_to(m[b:b + 1, None, j * d:(j + 1) * d], (b, 1, d)) for j in range(6)]
        mods.append((m_lat, m_ctx))

    def pre_mix(layer):
        m_lat, m_ctx = mods[layer]
        return ((norm_pre_mix[layer], m_lat[0], m_lat[1]), (norm_pre_mix[layer], m_ctx[0], m_ctx[1]))

    n0 = pre_mix(0)
    _, h_lat = norm_step(x, norm=n0[0])
    _, h_ctx = norm_step(ctx, norm=n0[1])
    for layer in range(depth):
        with_ctx = layer < depth - 1
        p = dict(w_in=w_in[layer], gdn_conv=gdn_conv[layer], gdn_a_log=gdn_a_log[layer],
                 gdn_dt_bias=gdn_dt_bias[layer], gdn_norm=gdn_norm[layer], swa_sink=swa_sink[layer],
                 diff_lambda=diff_lambda[layer], diff_norm=diff_norm[layer], w_out=w_out[layer],
                 norm_post_mix=norm_post_mix[layer], norm_pre_ffn=norm_pre_ffn[layer],
                 norm_post_ffn=norm_post_ffn[layer], ffn_w_gate=ffn_w_gate[layer],
                 ffn_w_up=ffn_w_up[layer], ffn_conv=ffn_conv[layer], ffn_w_down=ffn_w_down[layer])
        m_lat, m_ctx = mods[layer]
        next_norm = pre_mix(layer + 1) if layer + 1 < depth else None
        x, ctx, h_lat, h_ctx = _layer(x, ctx, h_lat, h_ctx, m_lat, m_ctx, p, layer, with_ctx,
                                      rope_tabs, next_norm)
    return x
```

```python
import functools
import math

import jax
import jax.numpy as jnp
import numpy as np
from jax import lax
from jax.experimental import pallas as pl
from jax.experimental.pallas import tpu as pltpu

F32 = jnp.float32
BF16 = jnp.bfloat16

GRID_W = 64
HEAD_DIM = 128
LANES = 128
EPS = 1e-6
ROPE_BASE = 10000.0

GDN_HEADS = 6
GDN_CONV = 5
GDN_CHUNK = 64
GDN_W = GDN_HEADS * HEAD_DIM
SWA_HEADS = 6
SWA_KV_HEADS = 2
SWA_GROUP = 3
SWA_WINDOW = 128
SWA_BLOCK = 128
DIFF_HEADS = 4
DIFF_QK_DIM = 64
FFN_CONV = 3

VMEM_LIMIT_BYTES = 56 * 1024 * 1024


def _params(*sem):
    return pltpu.CompilerParams(dimension_semantics=sem, vmem_limit_bytes=VMEM_LIMIT_BYTES)


def _silu(x):
    return x * jax.nn.sigmoid(x)


def _dot(a, b):
    return jnp.dot(a, b, preferred_element_type=F32)


def _mm_kernel(a_ref, b_ref, o_ref):
    o_ref[...] = _dot(a_ref[...], b_ref[...]).astype(o_ref.dtype)


def matmul(a, b, out_dtype, tm=2048, tn=512, name="matmul"):
    m, k = a.shape
    n = b.shape[1]
    tm = min(tm, m)
    tn = min(tn, n)
    assert m % tm == 0 and n % tn == 0, (m, n, tm, tn)
    return pl.pallas_call(
        _mm_kernel,
        grid=(m // tm, n // tn),
        in_specs=[pl.BlockSpec((tm, k), lambda i, j: (i, 0)),
                  pl.BlockSpec((k, tn), lambda i, j: (0, j))],
        out_specs=pl.BlockSpec((tm, tn), lambda i, j: (i, j)),
        out_shape=jax.ShapeDtypeStruct((m, n), out_dtype),
        compiler_params=_params("parallel", "parallel"),
        name=name,
    )(a, b)


def _mod_kernel(c_ref, w_ref, b_ref, o_ref):
    x = _silu(c_ref[...]).astype(BF16)
    o_ref[...] = jnp.dot(x, w_ref[...].astype(BF16), preferred_element_type=F32) + b_ref[...]


def modulation(cc, w, b):
    m, k = cc.shape
    n = w.shape[1]
    tn = 1024
    return pl.pallas_call(
        _mod_kernel,
        grid=(n // tn,),
        in_specs=[pl.BlockSpec((m, k), lambda j: (0, 0)),
                  pl.BlockSpec((k, tn), lambda j: (0, j)),
                  pl.BlockSpec((1, tn), lambda j: (0, j))],
        out_specs=pl.BlockSpec((m, tn), lambda j: (0, j)),
        out_shape=jax.ShapeDtypeStruct((m, n), F32),
        compiler_params=_params("parallel"),
        name="modulation",
    )(cc, w, b.reshape(1, n))


def _rms(x, g):
    return x * lax.rsqrt(jnp.mean(x * x, axis=-1, keepdims=True) + EPS) * g


def _norm_kernel(has_resid, has_norm, *refs):
    refs = list(refs)
    x = refs.pop(0)[0]
    if has_resid:
        y = refs.pop(0)[0]
        gate = refs.pop(0)[0]
        gpost = refs.pop(0)[...]
        x = x + gate * _rms(y, gpost)
    if has_norm:
        gpre = refs.pop(0)[...]
        shift = refs.pop(0)[0]
        scale = refs.pop(0)[0]
    if has_resid:
        refs.pop(0)[0] = x
    if has_norm:
        refs.pop(0)[0] = (_rms(x, gpre) * (1.0 + scale) + shift).astype(BF16)


def norm_step(x, resid=None, norm=None, ts=512):
    b, s, d = x.shape
    ts = min(ts, s)
    assert s % ts == 0
    tok = pl.BlockSpec((1, ts, d), lambda i, j: (i, j, 0))
    vec = pl.BlockSpec((1, 1, d), lambda i, j: (i, 0, 0))
    gsp = pl.BlockSpec((1, d), lambda i, j: (0, 0))
    args, specs, outs, out_specs = [x], [tok], [], []
    if resid is not None:
        y, gate, gpost = resid
        args += [y, gate, gpost.reshape(1, d)]
        specs += [tok, vec, gsp]
        outs.append(jax.ShapeDtypeStruct((b, s, d), F32))
        out_specs.append(tok)
    if norm is not None:
        gpre, shift, scale = norm
        args += [gpre.reshape(1, d), shift, scale]
        specs += [gsp, vec, vec]
        outs.append(jax.ShapeDtypeStruct((b, s, d), BF16))
        out_specs.append(tok)
    res = pl.pallas_call(
        functools.partial(_norm_kernel, resid is not None, norm is not None),
        grid=(b, s // ts),
        in_specs=specs, out_specs=out_specs, out_shape=outs,
        compiler_params=_params("parallel", "parallel"),
        name="norm_step",
    )(*args)
    res = list(res)
    x_new = res.pop(0) if resid is not None else None
    h = res.pop(0) if norm is not None else None
    return x_new, h


def _seq_conv(x, prev8, next8, w, halo):
    t = x.shape[0]
    xe = jnp.concatenate([prev8, x, next8], axis=0)
    acc = None
    for j in range(2 * halo + 1):
        off = 8 + j - halo
        term = xe[off:off + t] * w[j:j + 1]
        acc = term if acc is None else acc + term
    return acc


def _halo_specs(ts, rows, col_of):
    per = ts // rows

    def main(b, i, c):
        return (b, i, col_of(c))

    def prev(b, i, c):
        return (b, jnp.maximum(i * per - 1, 0), col_of(c))

    def nxt(nblk):
        def f(b, i, c):
            return (b, jnp.minimum((i + 1) * per, nblk * per - 1), col_of(c))
        return f

    return main, prev, nxt


def _ffn_up_kernel(tiles_per_seq, h_ref, hp_ref, hn_ref, wg_ref, wu_ref, cw_ref, o_ref):
    local = pl.program_id(0) % tiles_per_seq
    h = h_ref[...]
    wg = wg_ref[...]
    g = _dot(h, wg)
    u = _dot(h, wu_ref[...])
    prev8 = _dot(hp_ref[...], wg)[8:16] * (local > 0).astype(F32)
    next8 = _dot(hn_ref[...], wg)[0:8] * (local < tiles_per_seq - 1).astype(F32)
    y = _seq_conv(g, prev8, next8, cw_ref[...], FFN_CONV // 2)
    o_ref[...] = (_silu(y) * u).astype(BF16)


def ffn_up(h, w_gate, w_up, conv_w, seq, tm=1024, tn=512):
    m, k = h.shape
    f = w_gate.shape[1]
    tm = min(tm, seq)
    assert seq % tm == 0 and f % tn == 0 and tm % 16 == 0, (seq, f, tm, tn)
    per = tm // 16
    nrow16 = m // 16
    wpad = jnp.zeros((8, f), F32).at[:FFN_CONV].set(conv_w)
    return pl.pallas_call(
        functools.partial(_ffn_up_kernel, seq // tm),
        grid=(m // tm, f // tn),
        in_specs=[pl.BlockSpec((tm, k), lambda i, j: (i, 0)),
                  pl.BlockSpec((16, k), lambda i, j: (jnp.maximum(i * per - 1, 0), 0)),
                  pl.BlockSpec((16, k), lambda i, j: (jnp.minimum((i + 1) * per, nrow16 - 1), 0)),
                  pl.BlockSpec((k, tn), lambda i, j: (0, j)),
                  pl.BlockSpec((k, tn), lambda i, j: (0, j)),
                  pl.BlockSpec((8, tn), lambda i, j: (0, j))],
        out_specs=pl.BlockSpec((tm, tn), lambda i, j: (i, j)),
        out_shape=jax.ShapeDtypeStruct((m, f), BF16),
        compiler_params=_params("parallel", "parallel"),
        name="ffn_up",
    )(h, h, h, w_gate, w_up, wpad)


def _rope_tables(n_tokens, dim):
    quarter = dim // 4
    inv_freq = ROPE_BASE ** (-jnp.arange(quarter, dtype=F32) / quarter)
    tok = jnp.arange(n_tokens)
    rows = (tok // GRID_W).astype(F32)
    cols = (tok % GRID_W).astype(F32)
    ang_r = rows[:, None] * inv_freq
    ang_c = cols[:, None] * inv_freq
    ang = jnp.concatenate([ang_r, ang_r, ang_c, ang_c], axis=-1)
    ang = jnp.tile(ang, (1, LANES // dim))
    cos, sin = jnp.cos(ang), jnp.sin(ang)
    first = (jnp.arange(LANES) % (2 * quarter)) < quarter
    sin_a = jnp.where(first, -sin, 0.0)
    sin_b = jnp.where(first, 0.0, sin)
    return jnp.stack([cos, sin_a, sin_b])


def _rope_kernel(x_ref, tb_ref, tc_ref, o_ref):
    j = pl.program_id(2)
    ngroups = x_ref.shape[2] // LANES

    def apply(tab_ref, quarter):
        cos, sin_a, sin_b = tab_ref[0], tab_ref[1], tab_ref[2]
        for gidx in range(ngroups):
            sl = slice(gidx * LANES, (gidx + 1) * LANES)
            x = x_ref[0, :, sl]
            up = pltpu.roll(x, LANES - quarter, axis=1)
            dn = pltpu.roll(x, quarter, axis=1)
            o_ref[0, :, sl] = (x * cos + up * sin_a + dn * sin_b).astype(BF16)

    @pl.when(j == 0)
    def _():
        apply(tb_ref, HEAD_DIM // 4)

    @pl.when(j == 1)
    def _():
        apply(tc_ref, DIFF_QK_DIM // 4)


def rope_qk(x, tab_b, tab_c, ts=512):
    b, s, w = x.shape
    half = w // 2
    ts = min(ts, s)
    return pl.pallas_call(
        _rope_kernel,
        grid=(b, s // ts, 2),
        in_specs=[pl.BlockSpec((1, ts, half), lambda bb, i, j: (bb, i, j)),
                  pl.BlockSpec((3, ts, LANES), lambda bb, i, j: (0, i, 0)),
                  pl.BlockSpec((3, ts, LANES), lambda bb, i, j: (0, i, 0))],
        out_specs=pl.BlockSpec((1, ts, half), lambda bb, i, j: (bb, i, j)),
        out_shape=jax.ShapeDtypeStruct((b, s, w), BF16),
        compiler_params=_params("parallel", "parallel", "parallel"),
        name="rope_qk",
    )(x, tab_b, tab_c)


def _gdn_feat_kernel(x_ref, xp_ref, xn_ref, w_ref, o_ref):
    i = pl.program_id(1)
    kind = pl.program_id(2)
    last = pl.num_programs(1) - 1
    prev8 = xp_ref[0] * (i > 0).astype(F32)
    next8 = xn_ref[0] * (i < last).astype(F32)
    y = _silu(_seq_conv(x_ref[0], prev8, next8, w_ref[...], GDN_CONV // 2))
    for h in range(GDN_HEADS):
        hs = slice(h * HEAD_DIM, (h + 1) * HEAD_DIM)
        yh = y[:, hs]
        inv = lax.rsqrt(jnp.sum(yh * yh, axis=-1, keepdims=True) + EPS)
        inv = jnp.where(kind == 0, inv * (HEAD_DIM ** -0.5), inv)
        inv = jnp.where(kind == 2, jnp.ones_like(inv), inv)
        o_ref[0, :, hs] = yh * inv


def gdn_features(proj, conv_w, ts=512):
    b, l, _ = proj.shape
    ts = min(ts, l)
    nblk = l // ts
    main, prev, nxt = _halo_specs(ts, 8, lambda c: c)
    wpad = jnp.zeros((8, 3 * GDN_W), F32).at[:GDN_CONV].set(conv_w)
    return pl.pallas_call(
        _gdn_feat_kernel,
        grid=(b, nblk, 3),
        in_specs=[pl.BlockSpec((1, ts, GDN_W), main),
                  pl.BlockSpec((1, 8, GDN_W), prev),
                  pl.BlockSpec((1, 8, GDN_W), nxt(nblk)),
                  pl.BlockSpec((8, GDN_W), lambda bb, i, c: (0, c))],
        out_specs=pl.BlockSpec((1, ts, GDN_W), main),
        out_shape=jax.ShapeDtypeStruct((b, l, 3 * GDN_W), F32),
        compiler_params=_params("parallel", "parallel", "parallel"),
        name="gdn_features",
    )(proj, proj, proj, wpad)


def _split3(x):
    hi = x.astype(BF16)
    r = x - hi.astype(F32)
    mid = r.astype(BF16)
    lo = (r - mid.astype(F32)).astype(BF16)
    return hi, mid, lo


def _mask_dot(mask_bf16, x):
    hi, mid, lo = _split3(x)
    return _dot(mask_bf16, hi) + _dot(mask_bf16, mid) + _dot(mask_bf16, lo)


def _split2(x):
    hi = x.astype(BF16)
    lo = (x - hi.astype(F32)).astype(BF16)
    return hi, lo


def _dot3(a, b):
    return _dot(a[0], b[0]) + _dot(a[0], b[1]) + _dot(a[1], b[0])


def _unit_tri_inverse(a_list, same16, same32):
    n = a_list[0].shape[0]
    eye = (lax.broadcasted_iota(jnp.int32, (n, n), 0) == lax.broadcasted_iota(jnp.int32, (n, n), 1)).astype(F32)
    off16 = jnp.logical_and(same32, jnp.logical_not(same16))
    d = [jnp.where(same16, a, 0.0) for a in a_list]
    l1s = [_split2(jnp.where(off16, a, 0.0)) for a in a_list]
    l2s = [_split2(jnp.where(same32, 0.0, a)) for a in a_list]
    ds = [_split2(x) for x in d]
    d2 = [_dot3(x, x) for x in ds]
    d2s = [_split2(x) for x in d2]
    d3 = [_dot3(x2, x) for x2, x in zip(d2s, ds)]
    d4s = [_split2(_dot3(x2, x2)) for x2 in d2s]
    p = [eye - x + x2 - x3 for x, x2, x3 in zip(d, d2, d3)]
    ps = [_split2(x) for x in p]
    p = [x + _dot3(xs, x4) for x, xs, x4 in zip(p, ps, d4s)]
    d8s = [_split2(_dot3(x4, x4)) for x4 in d4s]
    ps = [_split2(x) for x in p]
    t = [x + _dot3(xs, x8) for x, xs, x8 in zip(p, ps, d8s)]
    for ls in (l1s, l2s):
        ts = [_split2(x) for x in t]
        xs = [_split2(_dot3(l, y)) for l, y in zip(ls, ts)]
        t = [x - _dot3(y, z) for x, y, z in zip(t, ts, xs)]
    return [_split2(x) for x in t]


def _gdn_kernel(reverse, q_ref, k_ref, v_ref, gt_ref, cst_ref, s0_ref, o_ref, sfin_ref,
                s_scr, g_scr, cum_scr, tot_scr, xt_scr):
    i = pl.program_id(1)
    nblk = pl.num_programs(1)
    tb = q_ref.shape[1]
    npairs = tb // 128
    d = 1 if reverse else 0
    heads = range(GDN_HEADS)

    @pl.when(i == 0)
    def _():
        s_scr[...] = s0_ref[0]

    raw = gt_ref[0]
    lane = lax.broadcasted_iota(jnp.int32, raw.shape, 1)
    neg_a = cst_ref[0:1, :]
    dt_bias = cst_ref[1:2, :]
    gates = jnp.where(lane < 2 * GDN_HEADS, jax.nn.sigmoid(raw), neg_a * jax.nn.softplus(raw + dt_bias))
    g_scr[...] = gates

    r64 = lax.broadcasted_iota(jnp.int32, (GDN_CHUNK, GDN_CHUNK), 0)
    c64 = lax.broadcasted_iota(jnp.int32, (GDN_CHUNK, GDN_CHUNK), 1)
    incl = (r64 <= c64) if reverse else (r64 >= c64)
    strict = (r64 < c64) if reverse else (r64 > c64)
    same16 = (r64 // 16) == (c64 // 16)
    same32 = (r64 // 32) == (c64 // 32)

    r128 = lax.broadcasted_iota(jnp.int32, (128, 128), 0)
    c128 = lax.broadcasted_iota(jnp.int32, (128, 128), 1)
    same_chunk = (r128 // GDN_CHUNK) == (c128 // GDN_CHUNK)
    order = (r128 <= c128) if reverse else (r128 >= c128)
    cum_mask = jnp.logical_and(same_chunk, order).astype(BF16)
    tot_mask = same_chunk.astype(BF16)
    for p in range(npairs):
        rows = slice(p * 128, (p + 1) * 128)
        gp = gates[rows]
        cum = _mask_dot(cum_mask, gp)
        tot = _mask_dot(tot_mask, gp)
        cum_scr[rows, :] = cum
        tot_scr[rows, :] = tot
        lane_p = lax.broadcasted_iota(jnp.int32, gp.shape, 1)
        xt_scr[p] = jnp.where(lane_p < 2 * GDN_HEADS, gp, cum).T

    chunk_order = (1, 0) if reverse else (0, 1)

    def pair_body(pp, carry):
        p = (npairs - 1 - pp) if reverse else pp
        row0 = pl.multiple_of(p * 128, 128)
        gp = g_scr[pl.ds(row0, 128), :]
        cump = cum_scr[pl.ds(row0, 128), :]
        totp = tot_scr[pl.ds(row0, 128), :]
        xtp = xt_scr[p]
        ecum = jnp.exp(cump)

        units = [(h, cc) for cc in chunk_order for h in heads]
        hsl = [slice(h * HEAD_DIM, (h + 1) * HEAD_DIM) for h in heads]
        qp = [q_ref[0, pl.ds(row0, 128), hsl[h]] for h in heads]
        kp = [k_ref[0, pl.ds(row0, 128), hsl[h]] for h in heads]
        vp = [v_ref[0, pl.ds(row0, 128), hsl[h]] for h in heads]
        ktp = [x.T for x in kp]

        def rs(cc):
            return slice(cc * GDN_CHUNK, (cc + 1) * GDN_CHUNK)

        def col(arr, cc, j):
            return arr[rs(cc), j:j + 1]

        bi = [d * GDN_HEADS + h for h in heads]
        gi = [2 * GDN_HEADS + d * GDN_HEADS + h for h in heads]
        bcol = [col(gp, cc, bi[h]) for h, cc in units]
        ecol = [col(ecum, cc, gi[h]) for h, cc in units]
        crow = [xtp[gi[h]:gi[h] + 1, rs(cc)] for h, cc in units]
        tot11 = [totp[cc * GDN_CHUNK:cc * GDN_CHUNK + 1, gi[h]:gi[h] + 1] for h, cc in units]
        decay = [jnp.exp(jnp.where(incl, col(cump, cc, gi[h]) - cr, -jnp.inf))
                 for (h, cc), cr in zip(units, crow)]
        kh = [kp[h][rs(cc)] for h, cc in units]
        qh = [qp[h][rs(cc)] for h, cc in units]
        vh = [vp[h][rs(cc)] for h, cc in units]
        kt = [ktp[h][:, rs(cc)] for h, cc in units]
        ktb = [x.astype(BF16) for x in kt]
        kk = [_dot(x.astype(BF16), y) for x, y in zip(kh, ktb)]
        qk = [_dot(x.astype(BF16), y) * dc for x, y, dc in zip(qh, ktb, decay)]
        a = [jnp.where(strict, b_ * x * dc, 0.0) for b_, x, dc in zip(bcol, kk, decay)]
        ts = _unit_tri_inverse(a, same16, same32)
        rhs = [_split2(jnp.concatenate([b_ * v_, (b_ * e_) * k_], axis=1))
               for b_, e_, v_, k_ in zip(bcol, ecol, vh, kh)]
        uw = [_dot3(t_, r_) for t_, r_ in zip(ts, rhs)]
        lhs_s = [jnp.concatenate([x[:, HEAD_DIM:], q_ * e_], axis=0).astype(BF16)
                 for x, q_, e_ in zip(uw, qh, ecol)]
        lhs_v = [jnp.concatenate([x, y * jnp.exp(t11 - cr)], axis=0).astype(BF16)
                 for x, y, t11, cr in zip(qk, kt, tot11, crow)]
        dec = [jnp.exp(x) for x in tot11]

        s = [s_scr[h] for h in heads]
        outs = {}
        for ci, cc in enumerate(chunk_order):
            un = [ci * GDN_HEADS + h for h in heads]
            sb = [x.astype(BF16) for x in s]
            rs_ = [_dot(lhs_s[j], sb[h]) for h, j in zip(heads, un)]
            vnb = [(uw[j][:, :HEAD_DIM] - r_[:GDN_CHUNK]).astype(BF16) for j, r_ in zip(un, rs_)]
            rv = [_dot(lhs_v[j], x) for j, x in zip(un, vnb)]
            for h, j in zip(heads, un):
                outs[(h, cc)] = rs_[h][GDN_CHUNK:] + rv[h][:GDN_CHUNK]
            s = [x * dec[j] + r_[GDN_CHUNK:] for x, j, r_ in zip(s, un, rv)]
        for h in heads:
            s_scr[h] = s[h]
            o_ref[0, pl.ds(row0, 128), hsl[h]] = jnp.concatenate([outs[(h, 0)], outs[(h, 1)]], axis=0)
        return carry

    lax.fori_loop(0, npairs, pair_body, 0)

    @pl.when(i == nblk - 1)
    def _():
        sfin_ref[0] = s_scr[...]


def gdn_scan(feat, gates, cst, s0, reverse, tb=512):
    b, l, _ = feat.shape
    tb = min(tb, l)
    nblk = l // tb
    gate_col = gates.shape[2] // LANES - 1

    def blk(i):
        return (nblk - 1 - i) if reverse else i

    def tok(col):
        return lambda bb, i: (bb, blk(i), col)

    st_spec = pl.BlockSpec((1, GDN_HEADS, HEAD_DIM, HEAD_DIM), lambda bb, i: (bb, 0, 0, 0))
    return pl.pallas_call(
        functools.partial(_gdn_kernel, reverse),
        grid=(b, nblk),
        in_specs=[pl.BlockSpec((1, tb, GDN_W), tok(0)),
                  pl.BlockSpec((1, tb, GDN_W), tok(1)),
                  pl.BlockSpec((1, tb, GDN_W), tok(2)),
                  pl.BlockSpec((1, tb, LANES), tok(gate_col)),
                  pl.BlockSpec((8, LANES), lambda bb, i: (0, 0)),
                  st_spec],
        out_specs=[pl.BlockSpec((1, tb, GDN_W), tok(0)), st_spec],
        out_shape=[jax.ShapeDtypeStruct((b, l, GDN_W), F32),
                   jax.ShapeDtypeStruct((b, GDN_HEADS, HEAD_DIM, HEAD_DIM), F32)],
        scratch_shapes=[pltpu.VMEM((GDN_HEADS, HEAD_DIM, HEAD_DIM), F32),
                        pltpu.VMEM((tb, LANES), F32),
                        pltpu.VMEM((tb, LANES), F32),
                        pltpu.VMEM((tb, LANES), F32),
                        pltpu.VMEM((tb // 128, LANES, 128), F32)],
        compiler_params=_params("parallel", "arbitrary"),
        name="gdn_scan_bwd" if reverse else "gdn_scan_fwd",
    )(feat, feat, feat, gates, cst, s0)


def _gdn_out_kernel(of_ref, ob_ref, z_ref, g_ref, o_ref):
    for h in range(GDN_HEADS):
        hs = slice(h * HEAD_DIM, (h + 1) * HEAD_DIM)
        o = of_ref[0, :, hs] + ob_ref[0, :, hs]
        o_ref[0, :, hs] = (_rms(o, g_ref[...]) * _silu(z_ref[0, :, hs])).astype(BF16)


def gdn_output(o_f, o_b, proj, norm_w, ts=512):
    b, l, w = o_f.shape
    ts = min(ts, l)
    tok = lambda bb, i: (bb, i, 0)
    return pl.pallas_call(
        _gdn_out_kernel,
        grid=(b, l // ts),
        in_specs=[pl.BlockSpec((1, ts, w), tok),
                  pl.BlockSpec((1, ts, w), tok),
                  pl.BlockSpec((1, ts, w), lambda bb, i: (bb, i, 3)),
                  pl.BlockSpec((1, HEAD_DIM), lambda bb, i: (0, 0))],
        out_specs=pl.BlockSpec((1, ts, w), tok),
        out_shape=jax.ShapeDtypeStruct((b, l, w), BF16),
        compiler_params=_params("parallel", "parallel"),
        name="gdn_output",
    )(o_f, o_b, proj, norm_w.reshape(1, HEAD_DIM))


def _dot_nt(a, b):
    return lax.dot_general(a, b, (((1,), (1,)), ((), ())), preferred_element_type=F32)


def _swa_kernel(windowed, nsub, sink_ref, q_ref, *refs):
    if windowed:
        kp_ref, kc_ref, kn_ref, vp_ref, vc_ref, vn_ref, kx_ref, vx_ref, o_ref = refs
    else:
        kx_ref, vx_ref, o_ref = refs
    hk = pl.program_id(1)
    i = pl.program_id(2)
    nb = pl.num_programs(2) * nsub
    scale = HEAD_DIM ** -0.5
    rows3 = SWA_GROUP * SWA_BLOCK
    kx = kx_ref[0]
    vx = vx_ref[0]
    sink = jnp.concatenate([jnp.full((SWA_BLOCK, 1), sink_ref[hk * SWA_GROUP + g], F32)
                            for g in range(SWA_GROUP)], axis=0)
    if windowed:
        kall = jnp.concatenate([kp_ref[0], kc_ref[0], kn_ref[0]], axis=0)
        vall = jnp.concatenate([vp_ref[0], vc_ref[0], vn_ref[0]], axis=0)
        r = lax.broadcasted_iota(jnp.int32, (rows3, 3 * SWA_BLOCK), 0) % SWA_BLOCK
        c = lax.broadcasted_iota(jnp.int32, (rows3, 3 * SWA_BLOCK), 1)
        band = jnp.abs(c - SWA_BLOCK - r) <= SWA_WINDOW
    for u in range(nsub):
        rows = slice(u * SWA_BLOCK, (u + 1) * SWA_BLOCK)
        q = jnp.concatenate([q_ref[0, rows, g * HEAD_DIM:(g + 1) * HEAD_DIM] for g in range(SWA_GROUP)], axis=0)
        s_ctx = _dot_nt(q, kx) * scale
        m = jnp.maximum(jnp.max(s_ctx, axis=-1, keepdims=True), sink)
        if windowed:
            blk = i * nsub + u
            mask = jnp.logical_and(band, jnp.logical_or(blk > 0, c >= SWA_BLOCK))
            mask = jnp.logical_and(mask, jnp.logical_or(blk < nb - 1, c < 2 * SWA_BLOCK))
            win = slice(u * SWA_BLOCK, (u + 3) * SWA_BLOCK)
            s_win = jnp.where(mask, _dot_nt(q, kall[win]) * scale, -jnp.inf)
            m = jnp.maximum(m, jnp.max(s_win, axis=-1, keepdims=True))
        e_ctx = jnp.exp(s_ctx - m)
        den = jnp.sum(e_ctx, axis=-1, keepdims=True) + jnp.exp(sink - m)
        if windowed:
            e_win = jnp.exp(s_win - m)
            den = den + jnp.sum(e_win, axis=-1, keepdims=True)
        inv = 1.0 / den
        o = _dot((e_ctx * inv).astype(BF16), vx)
        if windowed:
            o = o + _dot((e_win * inv).astype(BF16), vall[win])
        for g in range(SWA_GROUP):
            o_ref[0, rows, g * HEAD_DIM:(g + 1) * HEAD_DIM] = o[g * SWA_BLOCK:(g + 1) * SWA_BLOCK].astype(BF16)


def swa_attention(q, k, v, k_ctx, v_ctx, sink, windowed, tile=512):
    b, s, _ = q.shape
    tile = min(tile, s)
    assert s % tile == 0 and tile % SWA_BLOCK == 0, (s, tile)
    nsub = tile // SWA_BLOCK
    nb = s // SWA_BLOCK
    if windowed:
        (k_arr, k_col), (v_arr, v_col) = k, v
    (kx_arr, kx_col), (vx_arr, vx_col) = k_ctx, v_ctx
    n_ctx = kx_arr.shape[1]
    qspec = pl.BlockSpec((1, tile, SWA_GROUP * HEAD_DIM), lambda bb, hk, i: (bb, i, hk))

    def main(col):
        return pl.BlockSpec((1, tile, HEAD_DIM), lambda bb, hk, i: (bb, i, col + hk))

    def edge(col, nxt):
        return pl.BlockSpec((1, SWA_BLOCK, HEAD_DIM),
                            lambda bb, hk, i: (bb, jnp.clip((i + nxt) * nsub - 1 + nxt, 0, nb - 1), col + hk))

    def ctx(col):
        return pl.BlockSpec((1, n_ctx, HEAD_DIM), lambda bb, hk, i: (bb, 0, col + hk))

    specs = [pl.BlockSpec(memory_space=pltpu.SMEM), qspec]
    args = [sink, q]
    if windowed:
        specs += [edge(k_col, 0), main(k_col), edge(k_col, 1), edge(v_col, 0), main(v_col), edge(v_col, 1)]
        args += [k_arr] * 3 + [v_arr] * 3
    specs += [ctx(kx_col), ctx(vx_col)]
    args += [kx_arr, vx_arr]
    return pl.pallas_call(
        functools.partial(_swa_kernel, windowed, nsub),
        grid=(b, SWA_KV_HEADS, s // tile),
        in_specs=specs,
        out_specs=qspec,
        out_shape=jax.ShapeDtypeStruct((b, s, SWA_HEADS * HEAD_DIM), BF16),
        compiler_params=_params("parallel", "parallel", "parallel"),
        name="swa_attention",
    )(*args)


def _diff_kernel(kc, lam_init, lam_ref, nrm_ref, q_ref, k_ref, v_ref, o_ref, sa_ref, sb_ref):
    tq = q_ref.shape[1]
    nk = k_ref.shape[1] // kc
    q = q_ref[0]
    lane = lax.broadcasted_iota(jnp.int32, q.shape, 1)
    zero = jnp.zeros_like(q)
    qmaps = (jnp.where(lane < DIFF_QK_DIM, q, zero), jnp.where(lane >= DIFF_QK_DIM, q, zero))
    ones_col = (lax.broadcasted_iota(jnp.int32, (kc, HEAD_DIM), 1) == 0).astype(BF16)

    def logits(j, s_ref):
        kblk = k_ref[0, pl.ds(pl.multiple_of(j * kc, kc), kc), :]
        for mi in range(2):
            s_ref[mi] = _dot_nt(qmaps[mi], kblk).astype(BF16)

    def softmax_pv(j, s_ref, carry):
        row0 = pl.multiple_of(j * kc, kc)
        out = []
        for mi in range(2):
            m, acc = carry[2 * mi:2 * mi + 2]
            m_new = jnp.maximum(m, jnp.max(s_ref[mi], axis=-1, keepdims=True).astype(F32))
            acc = jnp.exp(m - m_new) * acc
            e = jnp.exp(s_ref[mi] - m_new.astype(BF16))
            v1 = jnp.concatenate([v_ref[0, pl.ds(row0, kc), :], ones_col], axis=1)
            acc = acc + _dot(e, v1)
            out += [m_new, acc]
        return tuple(out)

    logits(0, sa_ref)

    def body(t, carry):
        logits(2 * t + 1, sb_ref)
        carry = softmax_pv(2 * t, sa_ref, carry)
        logits(jnp.minimum(2 * t + 2, nk - 1), sa_ref)
        return softmax_pv(2 * t + 1, sb_ref, carry)

    neg = jnp.full((tq, 1), -jnp.inf, F32)
    zacc = jnp.zeros((tq, 2 * HEAD_DIM), F32)
    carry = lax.fori_loop(0, nk // 2, body, (neg, zacc, neg, zacc))
    if nk % 2:
        carry = softmax_pv(nk - 1, sa_ref, carry)
    _, a0, _, a1 = carry
    lam = lam_ref[...]
    lam_full = (jnp.exp(jnp.sum(lam[0:1] * lam[1:2], axis=-1, keepdims=True))
                - jnp.exp(jnp.sum(lam[2:3] * lam[3:4], axis=-1, keepdims=True)) + lam_init)
    o = (a0[:, :HEAD_DIM] / a0[:, HEAD_DIM:HEAD_DIM + 1]
         - lam_full * (a1[:, :HEAD_DIM] / a1[:, HEAD_DIM:HEAD_DIM + 1]))
    o_ref[0] = (_rms(o, nrm_ref[...]) * (1.0 - lam_init)).astype(BF16)


def diff_attention(q, k, v, lam, norm_w, lam_init, tq=512):
    (q_arr, q_col), (k_arr, k_col), (v_arr, v_col) = q, k, v
    b, s, _ = q_arr.shape
    sk = k_arr.shape[1]
    tq = min(tq, s)
    kc = 1280 if sk % 1280 == 0 else 256
    assert s % tq == 0 and sk % kc == 0, (s, sk, tq, kc)
    return pl.pallas_call(
        functools.partial(_diff_kernel, kc, lam_init),
        grid=(b, DIFF_HEADS, s // tq),
        in_specs=[pl.BlockSpec((4, DIFF_QK_DIM), lambda bb, h, i: (0, 0)),
                  pl.BlockSpec((1, HEAD_DIM), lambda bb, h, i: (0, 0)),
                  pl.BlockSpec((1, tq, HEAD_DIM), lambda bb, h, i: (bb, i, q_col + h)),
                  pl.BlockSpec((1, sk, HEAD_DIM), lambda bb, h, i: (bb, 0, k_col + h)),
                  pl.BlockSpec((1, sk, HEAD_DIM), lambda bb, h, i: (bb, 0, v_col + h))],
        out_specs=pl.BlockSpec((1, tq, HEAD_DIM), lambda bb, h, i: (bb, i, h)),
        out_shape=jax.ShapeDtypeStruct((b, s, DIFF_HEADS * HEAD_DIM), BF16),
        scratch_shapes=[pltpu.VMEM((2, tq, kc), BF16), pltpu.VMEM((2, tq, kc), BF16)],
        compiler_params=_params("parallel", "parallel", "parallel"),
        name="diff_attention",
    )(lam, norm_w.reshape(1, HEAD_DIM), q_arr, k_arr, v_arr)


_O_QKV, _O_Z, _O_BETA, _O_ALPHA, _O_SQ, _O_SK, _O_SV, _O_DQ, _O_DK, _O_DV, _O_END = (
    0, 2304, 3072, 3084, 3096, 3864, 4120, 4376, 4888, 5400, 5912)


def _split_w_in(w):
    gates = jnp.zeros((w.shape[0], LANES), w.dtype).at[:, :_O_SQ - _O_BETA].set(w[:, _O_BETA:_O_SQ])
    w_gdn = jnp.concatenate([w[:, _O_QKV:_O_BETA], gates], axis=1)
    dq = w[:, _O_DQ:_O_DK] * (DIFF_QK_DIM ** -0.5)
    w_qk = jnp.concatenate([w[:, _O_SQ:_O_SV], dq, w[:, _O_DK:_O_DV]], axis=1)
    w_v = jnp.concatenate([w[:, _O_SV:_O_DQ], w[:, _O_DV:_O_END]], axis=1)
    return w_gdn.astype(BF16), w_qk.astype(BF16), w_v.astype(BF16)


def _project(h, w_gdn, w_qk, w_v, qk_dtype):
    b, s, d = h.shape
    h2 = h.reshape(b * s, d)
    p_gdn = matmul(h2, w_gdn, F32, tn=640, name="proj_gdn").reshape(b, s, -1)
    p_qk = matmul(h2, w_qk, qk_dtype, name="proj_qk").reshape(b, s, -1)
    p_v = matmul(h2, w_v, BF16, tn=768, name="proj_v").reshape(b, s, -1)
    return p_gdn, p_qk, p_v


def _gdn_consts(a_log, dt_bias):
    cst = jnp.zeros((8, LANES), F32)
    cst = cst.at[0, 2 * GDN_HEADS:4 * GDN_HEADS].set(-jnp.exp(a_log.astype(F32)).reshape(-1))
    cst = cst.at[1, 2 * GDN_HEADS:4 * GDN_HEADS].set(dt_bias.astype(F32).reshape(-1))
    return cst


def _layer(x, ctx, h_lat, h_ctx, m_lat, m_ctx, p, layer, with_ctx, rope_tabs, next_norm):
    b, s, d = x.shape
    n_ctx = ctx.shape[1]
    w_gdn, w_qk, w_v = _split_w_in(p["w_in"])
    w_out = p["w_out"].astype(BF16)
    lg, lqk, lv = _project(h_lat, w_gdn, w_qk, w_v, F32)
    cg, cqk, cv = _project(h_ctx, w_gdn, w_qk, w_v, BF16)

    cst = _gdn_consts(p["gdn_a_log"], p["gdn_dt_bias"])
    f_lat = gdn_features(lg, p["gdn_conv"])
    f_ctx = gdn_features(cg, p["gdn_conv"])
    s0 = jnp.zeros((b, GDN_HEADS, HEAD_DIM, HEAD_DIM), F32)
    o_lat, o_ctx = [], []
    for reverse in (False, True):
        oc, s_ctx = gdn_scan(f_ctx, cg, cst, s0, reverse)
        ol, _ = gdn_scan(f_lat, lg, cst, s_ctx, reverse)
        o_lat.append(ol)
        o_ctx.append(oc)
    a_lat = gdn_output(o_lat[0], o_lat[1], lg, p["gdn_norm"])

    lqk = rope_qk(lqk, *rope_tabs)
    sink = p["swa_sink"].astype(F32)
    sq_c, sk_c, dq_c, dk_c = 0, SWA_HEADS, SWA_HEADS + SWA_KV_HEADS, SWA_HEADS + SWA_KV_HEADS + DIFF_HEADS
    sv_c, dv_c = 0, SWA_KV_HEADS
    b_lat = swa_attention(lqk, (lqk, sk_c), (lv, sv_c), (cqk, sk_c), (cv, sv_c), sink, True)
    lam_init = 0.8 - 0.6 * math.exp(-0.3 * layer)
    lam = p["diff_lambda"].astype(F32)
    dk_all = jnp.concatenate([lqk[:, :, dk_c * LANES:], cqk[:, :, dk_c * LANES:]], axis=1)
    dv_all = jnp.concatenate([lv[:, :, dv_c * LANES:], cv[:, :, dv_c * LANES:]], axis=1)
    c_lat = diff_attention((lqk, dq_c), (dk_all, 0), (dv_all, 0), lam, p["diff_norm"], lam_init)
    mix = jnp.concatenate([a_lat, b_lat, c_lat], axis=-1).reshape(b * s, d)
    y_lat = matmul(mix, w_out, F32, name="proj_out").reshape(b, s, d)

    x, hf_lat = norm_step(x, resid=(y_lat, m_lat[2], p["norm_post_mix"]),
                          norm=(p["norm_pre_ffn"], m_lat[3], m_lat[4]))
    if with_ctx:
        a_ctx = gdn_output(o_ctx[0], o_ctx[1], cg, p["gdn_norm"])
        b_ctx = swa_attention(cqk, None, None, (cqk, sk_c), (cv, sv_c), sink, False)
        c_ctx = diff_attention((cqk, dq_c), (cqk, dk_c), (cv, dv_c), lam, p["diff_norm"], lam_init)
        mix_c = jnp.concatenate([a_ctx, b_ctx, c_ctx], axis=-1).reshape(b * n_ctx, d)
        y_ctx = matmul(mix_c, w_out, F32, name="proj_out").reshape(b, n_ctx, d)
        ctx, hf_ctx = norm_step(ctx, resid=(y_ctx, m_ctx[2], p["norm_post_mix"]),
                                norm=(p["norm_pre_ffn"], m_ctx[3], m_ctx[4]))

    w_gate = p["ffn_w_gate"].astype(BF16)
    w_up = p["ffn_w_up"].astype(BF16)
    w_down = p["ffn_w_down"].astype(BF16)

    def ffn(h, n_tok):
        h2 = h.reshape(b * n_tok, d)
        act = ffn_up(h2, w_gate, w_up, p["ffn_conv"], n_tok)
        return matmul(act, w_down, F32, tm=1024, name="ffn_down").reshape(b, n_tok, d)

    f_lat = ffn(hf_lat, s)
    x, h_lat = norm_step(x, resid=(f_lat, m_lat[5], p["norm_post_ffn"]),
                         norm=None if next_norm is None else next_norm[0])
    h_ctx = None
    if with_ctx:
        f_ctx = ffn(hf_ctx, n_ctx)
        ctx, h_ctx = norm_step(ctx, resid=(f_ctx, m_ctx[5], p["norm_post_ffn"]),
                               norm=None if next_norm is None else next_norm[1])
    return x, ctx, h_lat, h_ctx


def kernel(x, c, ctx, c_ctx, w_mod, b_mod, norm_pre_mix, norm_post_mix, norm_pre_ffn, norm_post_ffn,
           w_in, gdn_conv, gdn_a_log, gdn_dt_bias, gdn_norm, swa_sink, diff_lambda, diff_norm, w_out,
           ffn_w_gate, ffn_w_up, ffn_conv, ffn_w_down):
    b, s, d = x.shape
    depth = w_mod.shape[0]
    rope_tabs = (_rope_tables(s, HEAD_DIM), _rope_tables(s, DIFF_QK_DIM))

    cc = jnp.zeros((8, d), F32).at[:b].set(c).at[b].set(c_ctx)
    mods = []
    for layer in range(depth):
        m = modulation(cc, w_mod[layer], b_mod[layer])
        m_lat = [m[:b, None, j * d:(j + 1) * d] for j in range(6)]
        m_ctx = [jnp.broadcast_to(m[b:b + 1, None, j * d:(j + 1) * d], (b, 1, d)) for j in range(6)]
        mods.append((m_lat, m_ctx))

    def pre_mix(layer):
        m_lat, m_ctx = mods[layer]
        return ((norm_pre_mix[layer], m_lat[0], m_lat[1]), (norm_pre_mix[layer], m_ctx[0], m_ctx[1]))

    n0 = pre_mix(0)
    _, h_lat = norm_step(x, norm=n0[0])
    _, h_ctx = norm_step(ctx, norm=n0[1])
    for layer in range(depth):
        with_ctx = layer < depth - 1
        p = dict(w_in=w_in[layer], gdn_conv=gdn_conv[layer], gdn_a_log=gdn_a_log[layer],
                 gdn_dt_bias=gdn_dt_bias[layer], gdn_norm=gdn_norm[layer], swa_sink=swa_sink[layer],
                 diff_lambda=diff_lambda[layer], diff_norm=diff_norm[layer], w_out=w_out[layer],
                 norm_post_mix=norm_post_mix[layer], norm_pre_ffn=norm_pre_ffn[layer],
                 norm_post_ffn=norm_post_ffn[layer], ffn_w_gate=ffn_w_gate[layer],
                 ffn_w_up=ffn_w_up[layer], ffn_conv=ffn_conv[layer], ffn_w_down=ffn_w_down[layer])
        m_lat, m_ctx = mods[layer]
        next_norm = pre_mix(layer + 1) if layer + 1 < depth else None
        x, ctx, h_lat, h_ctx = _layer(x, ctx, h_lat, h_ctx, m_lat, m_ctx, p, layer, with_ctx,
                                      rope_tabs, next_norm)
    return x
```

```python
import functools
import math

import jax
import jax.numpy as jnp
import numpy as np
from jax import lax
from jax.experimental import pallas as pl
from jax.experimental.pallas import tpu as pltpu

F32 = jnp.float32
BF16 = jnp.bfloat16

GRID_W = 64
HEAD_DIM = 128
LANES = 128
EPS = 1e-6
ROPE_BASE = 10000.0

GDN_HEADS = 6
GDN_CONV = 5
GDN_CHUNK = 64
GDN_W = GDN_HEADS * HEAD_DIM
SWA_HEADS = 6
SWA_KV_HEADS = 2
SWA_GROUP = 3
SWA_WINDOW = 128
SWA_BLOCK = 128
DIFF_HEADS = 4
DIFF_QK_DIM = 64
FFN_CONV = 3

VMEM_LIMIT_BYTES = 56 * 1024 * 1024


def _params(*sem):
    return pltpu.CompilerParams(dimension_semantics=sem, vmem_limit_bytes=VMEM_LIMIT_BYTES)


def _silu(x):
    return x * jax.nn.sigmoid(x)


def _dot(a, b):
    return jnp.dot(a, b, preferred_element_type=F32)


def _mm_kernel(a_ref, b_ref, o_ref):
    o_ref[...] = _dot(a_ref[...], b_ref[...]).astype(o_ref.dtype)


def matmul(a, b, out_dtype, tm=2048, tn=512, name="matmul"):
    m, k = a.shape
    n = b.shape[1]
    tm = min(tm, m)
    tn = min(tn, n)
    assert m % tm == 0 and n % tn == 0, (m, n, tm, tn)
    return pl.pallas_call(
        _mm_kernel,
        grid=(m // tm, n // tn),
        in_specs=[pl.BlockSpec((tm, k), lambda i, j: (i, 0)),
                  pl.BlockSpec((k, tn), lambda i, j: (0, j))],
        out_specs=pl.BlockSpec((tm, tn), lambda i, j: (i, j)),
        out_shape=jax.ShapeDtypeStruct((m, n), out_dtype),
        compiler_params=_params("parallel", "parallel"),
        name=name,
    )(a, b)


def _mod_kernel(c_ref, w_ref, b_ref, o_ref):
    x = _silu(c_ref[...]).astype(BF16)
    o_ref[...] = jnp.dot(x, w_ref[...].astype(BF16), preferred_element_type=F32) + b_ref[...]


def modulation(cc, w, b):
    m, k = cc.shape
    n = w.shape[1]
    tn = 1024
    return pl.pallas_call(
        _mod_kernel,
        grid=(n // tn,),
        in_specs=[pl.BlockSpec((m, k), lambda j: (0, 0)),
                  pl.BlockSpec((k, tn), lambda j: (0, j)),
                  pl.BlockSpec((1, tn), lambda j: (0, j))],
        out_specs=pl.BlockSpec((m, tn), lambda j: (0, j)),
        out_shape=jax.ShapeDtypeStruct((m, n), F32),
        compiler_params=_params("parallel"),
        name="modulation",
    )(cc, w, b.reshape(1, n))


def _rms(x, g):
    return x * lax.rsqrt(jnp.mean(x * x, axis=-1, keepdims=True) + EPS) * g


def _norm_kernel(has_resid, has_norm, *refs):
    refs = list(refs)
    x = refs.pop(0)[0]
    if has_resid:
        y = refs.pop(0)[0]
        gate = refs.pop(0)[0]
        gpost = refs.pop(0)[...]
        x = x + gate * _rms(y, gpost)
    if has_norm:
        gpre = refs.pop(0)[...]
        shift = refs.pop(0)[0]
        scale = refs.pop(0)[0]
    if has_resid:
        refs.pop(0)[0] = x
    if has_norm:
        refs.pop(0)[0] = (_rms(x, gpre) * (1.0 + scale) + shift).astype(BF16)


def norm_step(x, resid=None, norm=None, ts=512):
    b, s, d = x.shape
    ts = min(ts, s)
    assert s % ts == 0
    tok = pl.BlockSpec((1, ts, d), lambda i, j: (i, j, 0))
    vec = pl.BlockSpec((1, 1, d), lambda i, j: (i, 0, 0))
    gsp = pl.BlockSpec((1, d), lambda i, j: (0, 0))
    args, specs, outs, out_specs = [x], [tok], [], []
    if resid is not None:
        y, gate, gpost = resid
        args += [y, gate, gpost.reshape(1, d)]
        specs += [tok, vec, gsp]
        outs.append(jax.ShapeDtypeStruct((b, s, d), F32))
        out_specs.append(tok)
    if norm is not None:
        gpre, shift, scale = norm
        args += [gpre.reshape(1, d), shift, scale]
        specs += [gsp, vec, vec]
        outs.append(jax.ShapeDtypeStruct((b, s, d), BF16))
        out_specs.append(tok)
    res = pl.pallas_call(
        functools.partial(_norm_kernel, resid is not None, norm is not None),
        grid=(b, s // ts),
        in_specs=specs, out_specs=out_specs, out_shape=outs,
        compiler_params=_params("parallel", "parallel"),
        name="norm_step",
    )(*args)
    res = list(res)
    x_new = res.pop(0) if resid is not None else None
    h = res.pop(0) if norm is not None else None
    return x_new, h


def _seq_conv(x, prev8, next8, w, halo):
    t = x.shape[0]
    xe = jnp.concatenate([prev8, x, next8], axis=0)
    acc = None
    for j in range(2 * halo + 1):
        off = 8 + j - halo
        term = xe[off:off + t] * w[j:j + 1]
        acc = term if acc is None else acc + term
    return acc


def _halo_specs(ts, rows, col_of):
    per = ts // rows

    def main(b, i, c):
        return (b, i, col_of(c))

    def prev(b, i, c):
        return (b, jnp.maximum(i * per - 1, 0), col_of(c))

    def nxt(nblk):
        def f(b, i, c):
            return (b, jnp.minimum((i + 1) * per, nblk * per - 1), col_of(c))
        return f

    return main, prev, nxt


def _ffn_up_kernel(tiles_per_seq, h_ref, hp_ref, hn_ref, wg_ref, wu_ref, cw_ref, o_ref):
    local = pl.program_id(0) % tiles_per_seq
    h = h_ref[...]
    wg = wg_ref[...]
    g = _dot(h, wg)
    u = _dot(h, wu_ref[...])
    prev8 = _dot(hp_ref[...], wg)[8:16] * (local > 0).astype(F32)
    next8 = _dot(hn_ref[...], wg)[0:8] * (local < tiles_per_seq - 1).astype(F32)
    y = _seq_conv(g, prev8, next8, cw_ref[...], FFN_CONV // 2)
    o_ref[...] = (_silu(y) * u).astype(BF16)


def ffn_up(h, w_gate, w_up, conv_w, seq, tm=1024, tn=512):
    m, k = h.shape
    f = w_gate.shape[1]
    tm = min(tm, seq)
    assert seq % tm == 0 and f % tn == 0 and tm % 16 == 0, (seq, f, tm, tn)
    per = tm // 16
    nrow16 = m // 16
    wpad = jnp.zeros((8, f), F32).at[:FFN_CONV].set(conv_w)
    return pl.pallas_call(
        functools.partial(_ffn_up_kernel, seq // tm),
        grid=(m // tm, f // tn),
        in_specs=[pl.BlockSpec((tm, k), lambda i, j: (i, 0)),
                  pl.BlockSpec((16, k), lambda i, j: (jnp.maximum(i * per - 1, 0), 0)),
                  pl.BlockSpec((16, k), lambda i, j: (jnp.minimum((i + 1) * per, nrow16 - 1), 0)),
                  pl.BlockSpec((k, tn), lambda i, j: (0, j)),
                  pl.BlockSpec((k, tn), lambda i, j: (0, j)),
                  pl.BlockSpec((8, tn), lambda i, j: (0, j))],
        out_specs=pl.BlockSpec((tm, tn), lambda i, j: (i, j)),
        out_shape=jax.ShapeDtypeStruct((m, f), BF16),
        compiler_params=_params("parallel", "parallel"),
        name="ffn_up",
    )(h, h, h, w_gate, w_up, wpad)


def _rope_tables(n_tokens, dim):
    quarter = dim // 4
    inv_freq = ROPE_BASE ** (-jnp.arange(quarter, dtype=F32) / quarter)
    tok = jnp.arange(n_tokens)
    rows = (tok // GRID_W).astype(F32)
    cols = (tok % GRID_W).astype(F32)
    ang_r = rows[:, None] * inv_freq
    ang_c = cols[:, None] * inv_freq
    ang = jnp.concatenate([ang_r, ang_r, ang_c, ang_c], axis=-1)
    ang = jnp.tile(ang, (1, LANES // dim))
    cos, sin = jnp.cos(ang), jnp.sin(ang)
    first = (jnp.arange(LANES) % (2 * quarter)) < quarter
    sin_a = jnp.where(first, -sin, 0.0)
    sin_b = jnp.where(first, 0.0, sin)
    return jnp.stack([cos, sin_a, sin_b])


def _rope_kernel(x_ref, tb_ref, tc_ref, o_ref):
    j = pl.program_id(2)
    ngroups = x_ref.shape[2] // LANES

    def apply(tab_ref, quarter):
        cos, sin_a, sin_b = tab_ref[0], tab_ref[1], tab_ref[2]
        for gidx in range(ngroups):
            sl = slice(gidx * LANES, (gidx + 1) * LANES)
            x = x_ref[0, :, sl]
            up = pltpu.roll(x, LANES - quarter, axis=1)
            dn = pltpu.roll(x, quarter, axis=1)
            o_ref[0, :, sl] = (x * cos + up * sin_a + dn * sin_b).astype(BF16)

    @pl.when(j == 0)
    def _():
        apply(tb_ref, HEAD_DIM // 4)

    @pl.when(j == 1)
    def _():
        apply(tc_ref, DIFF_QK_DIM // 4)


def rope_qk(x, tab_b, tab_c, ts=512):
    b, s, w = x.shape
    half = w // 2
    ts = min(ts, s)
    return pl.pallas_call(
        _rope_kernel,
        grid=(b, s // ts, 2),
        in_specs=[pl.BlockSpec((1, ts, half), lambda bb, i, j: (bb, i, j)),
                  pl.BlockSpec((3, ts, LANES), lambda bb, i, j: (0, i, 0)),
                  pl.BlockSpec((3, ts, LANES), lambda bb, i, j: (0, i, 0))],
        out_specs=pl.BlockSpec((1, ts, half), lambda bb, i, j: (bb, i, j)),
        out_shape=jax.ShapeDtypeStruct((b, s, w), BF16),
        compiler_params=_params("parallel", "parallel", "parallel"),
        name="rope_qk",
    )(x, tab_b, tab_c)


def _gdn_feat_kernel(x_ref, xp_ref, xn_ref, w_ref, o_ref):
    i = pl.program_id(1)
    kind = pl.program_id(2)
    last = pl.num_programs(1) - 1
    prev8 = xp_ref[0] * (i > 0).astype(F32)
    next8 = xn_ref[0] * (i < last).astype(F32)
    y = _silu(_seq_conv(x_ref[0], prev8, next8, w_ref[...], GDN_CONV // 2))
    for h in range(GDN_HEADS):
        hs = slice(h * HEAD_DIM, (h + 1) * HEAD_DIM)
        yh = y[:, hs]
        inv = lax.rsqrt(jnp.sum(yh * yh, axis=-1, keepdims=True) + EPS)
        inv = jnp.where(kind == 0, inv * (HEAD_DIM ** -0.5), inv)
        inv = jnp.where(kind == 2, jnp.ones_like(inv), inv)
        o_ref[0, :, hs] = yh * inv


def gdn_features(proj, conv_w, ts=512):
    b, l, _ = proj.shape
    ts = min(ts, l)
    nblk = l // ts
    main, prev, nxt = _halo_specs(ts, 8, lambda c: c)
    wpad = jnp.zeros((8, 3 * GDN_W), F32).at[:GDN_CONV].set(conv_w)
    return pl.pallas_call(
        _gdn_feat_kernel,
        grid=(b, nblk, 3),
        in_specs=[pl.BlockSpec((1, ts, GDN_W), main),
                  pl.BlockSpec((1, 8, GDN_W), prev),
                  pl.BlockSpec((1, 8, GDN_W), nxt(nblk)),
                  pl.BlockSpec((8, GDN_W), lambda bb, i, c: (0, c))],
        out_specs=pl.BlockSpec((1, ts, GDN_W), main),
        out_shape=jax.ShapeDtypeStruct((b, l, 3 * GDN_W), F32),
        compiler_params=_params("parallel", "parallel", "parallel"),
        name="gdn_features",
    )(proj, proj, proj, wpad)


def _split3(x):
    hi = x.astype(BF16)
    r = x - hi.astype(F32)
    mid = r.astype(BF16)
    lo = (r - mid.astype(F32)).astype(BF16)
    return hi, mid, lo


def _mask_dot(mask_bf16, x):
    hi, mid, lo = _split3(x)
    return _dot(mask_bf16, hi) + _dot(mask_bf16, mid) + _dot(mask_bf16, lo)


def _split2(x):
    hi = x.astype(BF16)
    lo = (x - hi.astype(F32)).astype(BF16)
    return hi, lo


def _dot3(a, b):
    return _dot(a[0], b[0]) + _dot(a[0], b[1]) + _dot(a[1], b[0])


def _unit_tri_inverse(a_list, same16, same32):
    n = a_list[0].shape[0]
    eye = (lax.broadcasted_iota(jnp.int32, (n, n), 0) == lax.broadcasted_iota(jnp.int32, (n, n), 1)).astype(F32)
    off16 = jnp.logical_and(same32, jnp.logical_not(same16))
    d = [jnp.where(same16, a, 0.0) for a in a_list]
    l1s = [_split2(jnp.where(off16, a, 0.0)) for a in a_list]
    l2s = [_split2(jnp.where(same32, 0.0, a)) for a in a_list]
    ds = [_split2(x) for x in d]
    d2 = [_dot3(x, x) for x in ds]
    d2s = [_split2(x) for x in d2]
    d3 = [_dot3(x2, x) for x2, x in zip(d2s, ds)]
    d4s = [_split2(_dot3(x2, x2)) for x2 in d2s]
    p = [eye - x + x2 - x3 for x, x2, x3 in zip(d, d2, d3)]
    ps = [_split2(x) for x in p]
    p = [x + _dot3(xs, x4) for x, xs, x4 in zip(p, ps, d4s)]
    d8s = [_split2(_dot3(x4, x4)) for x4 in d4s]
    ps = [_split2(x) for x in p]
    t = [x + _dot3(xs, x8) for x, xs, x8 in zip(p, ps, d8s)]
    for ls in (l1s, l2s):
        ts = [_split2(x) for x in t]
        xs = [_split2(_dot3(l, y)) for l, y in zip(ls, ts)]
        t = [x - _dot3(y, z) for x, y, z in zip(t, ts, xs)]
    return [_split2(x) for x in t]


def _gdn_kernel(reverse, q_ref, k_ref, v_ref, gt_ref, cst_ref, s0_ref, o_ref, sfin_ref,
                s_scr, g_scr, cum_scr, tot_scr, xt_scr):
    i = pl.program_id(1)
    nblk = pl.num_programs(1)
    tb = q_ref.shape[1]
    npairs = tb // 128
    d = 1 if reverse else 0
    heads = range(GDN_HEADS)

    @pl.when(i == 0)
    def _():
        s_scr[...] = s0_ref[0]

    raw = gt_ref[0]
    lane = lax.broadcasted_iota(jnp.int32, raw.shape, 1)
    neg_a = cst_ref[0:1, :]
    dt_bias = cst_ref[1:2, :]
    gates = jnp.where(lane < 2 * GDN_HEADS, jax.nn.sigmoid(raw), neg_a * jax.nn.softplus(raw + dt_bias))
    g_scr[...] = gates

    r64 = lax.broadcasted_iota(jnp.int32, (GDN_CHUNK, GDN_CHUNK), 0)
    c64 = lax.broadcasted_iota(jnp.int32, (GDN_CHUNK, GDN_CHUNK), 1)
    incl = (r64 <= c64) if reverse else (r64 >= c64)
    strict = (r64 < c64) if reverse else (r64 > c64)
    same16 = (r64 // 16) == (c64 // 16)
    same32 = (r64 // 32) == (c64 // 32)

    r128 = lax.broadcasted_iota(jnp.int32, (128, 128), 0)
    c128 = lax.broadcasted_iota(jnp.int32, (128, 128), 1)
    same_chunk = (r128 // GDN_CHUNK) == (c128 // GDN_CHUNK)
    order = (r128 <= c128) if reverse else (r128 >= c128)
    cum_mask = jnp.logical_and(same_chunk, order).astype(BF16)
    tot_mask = same_chunk.astype(BF16)
    for p in range(npairs):
        rows = slice(p * 128, (p + 1) * 128)
        gp = gates[rows]
        cum = _mask_dot(cum_mask, gp)
        tot = _mask_dot(tot_mask, gp)
        cum_scr[rows, :] = cum
        tot_scr[rows, :] = tot
        lane_p = lax.broadcasted_iota(jnp.int32, gp.shape, 1)
        xt_scr[p] = jnp.where(lane_p < 2 * GDN_HEADS, gp, cum).T

    chunk_order = (1, 0) if reverse else (0, 1)

    def pair_body(pp, carry):
        p = (npairs - 1 - pp) if reverse else pp
        row0 = pl.multiple_of(p * 128, 128)
        gp = g_scr[pl.ds(row0, 128), :]
        cump = cum_scr[pl.ds(row0, 128), :]
        totp = tot_scr[pl.ds(row0, 128), :]
        xtp = xt_scr[p]
        ecum = jnp.exp(cump)

        units = [(h, cc) for cc in chunk_order for h in heads]
        hsl = [slice(h * HEAD_DIM, (h + 1) * HEAD_DIM) for h in heads]
        qp = [q_ref[0, pl.ds(row0, 128), hsl[h]] for h in heads]
        kp = [k_ref[0, pl.ds(row0, 128), hsl[h]] for h in heads]
        vp = [v_ref[0, pl.ds(row0, 128), hsl[h]] for h in heads]
        ktp = [x.T for x in kp]

        def rs(cc):
            return slice(cc * GDN_CHUNK, (cc + 1) * GDN_CHUNK)

        def col(arr, cc, j):
            return arr[rs(cc), j:j + 1]

        bi = [d * GDN_HEADS + h for h in heads]
        gi = [2 * GDN_HEADS + d * GDN_HEADS + h for h in heads]
        bcol = [col(gp, cc, bi[h]) for h, cc in units]
        ecol = [col(ecum, cc, gi[h]) for h, cc in units]
        crow = [xtp[gi[h]:gi[h] + 1, rs(cc)] for h, cc in units]
        tot11 = [totp[cc * GDN_CHUNK:cc * GDN_CHUNK + 1, gi[h]:gi[h] + 1] for h, cc in units]
        decay = [jnp.exp(jnp.where(incl, col(cump, cc, gi[h]) - cr, -jnp.inf))
                 for (h, cc), cr in zip(units, crow)]
        kh = [kp[h][rs(cc)] for h, cc in units]
        qh = [qp[h][rs(cc)] for h, cc in units]
        vh = [vp[h][rs(cc)] for h, cc in units]
        kt = [ktp[h][:, rs(cc)] for h, cc in units]
        ktb = [x.astype(BF16) for x in kt]
        kk = [_dot(x.astype(BF16), y) for x, y in zip(kh, ktb)]
        qk = [_dot(x.astype(BF16), y) * dc for x, y, dc in zip(qh, ktb, decay)]
        a = [jnp.where(strict, b_ * x * dc, 0.0) for b_, x, dc in zip(bcol, kk, decay)]
        ts = _unit_tri_inverse(a, same16, same32)
        rhs = [_split2(jnp.concatenate([b_ * v_, (b_ * e_) * k_], axis=1))
               for b_, e_, v_, k_ in zip(bcol, ecol, vh, kh)]
        uw = [_dot3(t_, r_) for t_, r_ in zip(ts, rhs)]
        lhs_s = [jnp.concatenate([x[:, HEAD_DIM:], q_ * e_], axis=0).astype(BF16)
                 for x, q_, e_ in zip(uw, qh, ecol)]
        lhs_v = [jnp.concatenate([x, y * jnp.exp(t11 - cr)], axis=0).astype(BF16)
                 for x, y, t11, cr in zip(qk, kt, tot11, crow)]
        dec = [jnp.exp(x) for x in tot11]

        s = [s_scr[h] for h in heads]
        outs = {}
        for ci, cc in enumerate(chunk_order):
            un = [ci * GDN_HEADS + h for h in heads]
            sb = [x.astype(BF16) for x in s]
            rs_ = [_dot(lhs_s[j], sb[h]) for h, j in zip(heads, un)]
            vnb = [(uw[j][:, :HEAD_DIM] - r_[:GDN_CHUNK]).astype(BF16) for j, r_ in zip(un, rs_)]
            rv = [_dot(lhs_v[j], x) for j, x in zip(un, vnb)]
            for h, j in zip(heads, un):
                outs[(h, cc)] = rs_[h][GDN_CHUNK:] + rv[h][:GDN_CHUNK]
            s = [x * dec[j] + r_[GDN_CHUNK:] for x, j, r_ in zip(s, un, rv)]
        for h in heads:
            s_scr[h] = s[h]
            o_ref[0, pl.ds(row0, 128), hsl[h]] = jnp.concatenate([outs[(h, 0)], outs[(h, 1)]], axis=0)
        return carry

    lax.fori_loop(0, npairs, pair_body, 0)

    @pl.when(i == nblk - 1)
    def _():
        sfin_ref[0] = s_scr[...]


def gdn_scan(feat, gates, cst, s0, reverse, tb=512):
    b, l, _ = feat.shape
    tb = min(tb, l)
    nblk = l // tb
    gate_col = gates.shape[2] // LANES - 1

    def blk(i):
        return (nblk - 1 - i) if reverse else i

    def tok(col):
        return lambda bb, i: (bb, blk(i), col)

    st_spec = pl.BlockSpec((1, GDN_HEADS, HEAD_DIM, HEAD_DIM), lambda bb, i: (bb, 0, 0, 0))
    return pl.pallas_call(
        functools.partial(_gdn_kernel, reverse),
        grid=(b, nblk),
        in_specs=[pl.BlockSpec((1, tb, GDN_W), tok(0)),
                  pl.BlockSpec((1, tb, GDN_W), tok(1)),
                  pl.BlockSpec((1, tb, GDN_W), tok(2)),
                  pl.BlockSpec((1, tb, LANES), tok(gate_col)),
                  pl.BlockSpec((8, LANES), lambda bb, i: (0, 0)),
                  st_spec],
        out_specs=[pl.BlockSpec((1, tb, GDN_W), tok(0)), st_spec],
        out_shape=[jax.ShapeDtypeStruct((b, l, GDN_W), F32),
                   jax.ShapeDtypeStruct((b, GDN_HEADS, HEAD_DIM, HEAD_DIM), F32)],
        scratch_shapes=[pltpu.VMEM((GDN_HEADS, HEAD_DIM, HEAD_DIM), F32),
                        pltpu.VMEM((tb, LANES), F32),
                        pltpu.VMEM((tb, LANES), F32),
                        pltpu.VMEM((tb, LANES), F32),
                        pltpu.VMEM((tb // 128, LANES, 128), F32)],
        compiler_params=_params("parallel", "arbitrary"),
        name="gdn_scan_bwd" if reverse else "gdn_scan_fwd",
    )(feat, feat, feat, gates, cst, s0)


def _gdn_out_kernel(of_ref, ob_ref, z_ref, g_ref, o_ref):
    for h in range(GDN_HEADS):
        hs = slice(h * HEAD_DIM, (h + 1) * HEAD_DIM)
        o = of_ref[0, :, hs] + ob_ref[0, :, hs]
        o_ref[0, :, hs] = (_rms(o, g_ref[...]) * _silu(z_ref[0, :, hs])).astype(BF16)


def gdn_output(o_f, o_b, proj, norm_w, ts=512):
    b, l, w = o_f.shape
    ts = min(ts, l)
    tok = lambda bb, i: (bb, i, 0)
    return pl.pallas_call(
        _gdn_out_kernel,
        grid=(b, l // ts),
        in_specs=[pl.BlockSpec((1, ts, w), tok),
                  pl.BlockSpec((1, ts, w), tok),
                  pl.BlockSpec((1, ts, w), lambda bb, i: (bb, i, 3)),
                  pl.BlockSpec((1, HEAD_DIM), lambda bb, i: (0, 0))],
        out_specs=pl.BlockSpec((1, ts, w), tok),
        out_shape=jax.ShapeDtypeStruct((b, l, w), BF16),
        compiler_params=_params("parallel", "parallel"),
        name="gdn_output",
    )(o_f, o_b, proj, norm_w.reshape(1, HEAD_DIM))


def _dot_nt(a, b):
    return lax.dot_general(a, b, (((1,), (1,)), ((), ())), preferred_element_type=F32)


def _swa_kernel(windowed, nsub, sink_ref, q_ref, *refs):
    if windowed:
        kp_ref, kc_ref, kn_ref, vp_ref, vc_ref, vn_ref, kx_ref, vx_ref, o_ref = refs
    else:
        kx_ref, vx_ref, o_ref = refs
    hk = pl.program_id(1)
    i = pl.program_id(2)
    nb = pl.num_programs(2) * nsub
    scale = HEAD_DIM ** -0.5
    rows3 = SWA_GROUP * SWA_BLOCK
    kx = kx_ref[0]
    vx = vx_ref[0]
    sink = jnp.concatenate([jnp.full((SWA_BLOCK, 1), sink_ref[hk * SWA_GROUP + g], F32)
                            for g in range(SWA_GROUP)], axis=0)
    if windowed:
        kall = jnp.concatenate([kp_ref[0], kc_ref[0], kn_ref[0]], axis=0)
        vall = jnp.concatenate([vp_ref[0], vc_ref[0], vn_ref[0]], axis=0)
        r = lax.broadcasted_iota(jnp.int32, (rows3, 3 * SWA_BLOCK), 0) % SWA_BLOCK
        c = lax.broadcasted_iota(jnp.int32, (rows3, 3 * SWA_BLOCK), 1)
        band = jnp.abs(c - SWA_BLOCK - r) <= SWA_WINDOW
    for u in range(nsub):
        rows = slice(u * SWA_BLOCK, (u + 1) * SWA_BLOCK)
        q = jnp.concatenate([q_ref[0, rows, g * HEAD_DIM:(g + 1) * HEAD_DIM] for g in range(SWA_GROUP)], axis=0)
        s_ctx = _dot_nt(q, kx) * scale
        m = jnp.maximum(jnp.max(s_ctx, axis=-1, keepdims=True), sink)
        if windowed:
            blk = i * nsub + u
            mask = jnp.logical_and(band, jnp.logical_or(blk > 0, c >= SWA_BLOCK))
            mask = jnp.logical_and(mask, jnp.logical_or(blk < nb - 1, c < 2 * SWA_BLOCK))
            win = slice(u * SWA_BLOCK, (u + 3) * SWA_BLOCK)
            s_win = jnp.where(mask, _dot_nt(q, kall[win]) * scale, -jnp.inf)
            m = jnp.maximum(m, jnp.max(s_win, axis=-1, keepdims=True))
        e_ctx = jnp.exp(s_ctx - m)
        den = jnp.sum(e_ctx, axis=-1, keepdims=True) + jnp.exp(sink - m)
        if windowed:
            e_win = jnp.exp(s_win - m)
            den = den + jnp.sum(e_win, axis=-1, keepdims=True)
        inv = 1.0 / den
        o = _dot((e_ctx * inv).astype(BF16), vx)
        if windowed:
            o = o + _dot((e_win * inv).astype(BF16), vall[win])
        for g in range(SWA_GROUP):
            o_ref[0, rows, g * HEAD_DIM:(g + 1) * HEAD_DIM] = o[g * SWA_BLOCK:(g + 1) * SWA_BLOCK].astype(BF16)


def swa_attention(q, k, v, k_ctx, v_ctx, sink, windowed, tile=512):
    b, s, _ = q.shape
    tile = min(tile, s)
    assert s % tile == 0 and tile % SWA_BLOCK == 0, (s, tile)
    nsub = tile // SWA_BLOCK
    nb = s // SWA_BLOCK
    if windowed:
        (k_arr, k_col), (v_arr, v_col) = k, v
    (kx_arr, kx_col), (vx_arr, vx_col) = k_ctx, v_ctx
    n_ctx = kx_arr.shape[1]
    qspec = pl.BlockSpec((1, tile, SWA_GROUP * HEAD_DIM), lambda bb, hk, i: (bb, i, hk))

    def main(col):
        return pl.BlockSpec((1, tile, HEAD_DIM), lambda bb, hk, i: (bb, i, col + hk))

    def edge(col, nxt):
        return pl.BlockSpec((1, SWA_BLOCK, HEAD_DIM),
                            lambda bb, hk, i: (bb, jnp.clip((i + nxt) * nsub - 1 + nxt, 0, nb - 1), col + hk))

    def ctx(col):
        return pl.BlockSpec((1, n_ctx, HEAD_DIM), lambda bb, hk, i: (bb, 0, col + hk))

    specs = [pl.BlockSpec(memory_space=pltpu.SMEM), qspec]
    args = [sink, q]
    if windowed:
        specs += [edge(k_col, 0), main(k_col), edge(k_col, 1), edge(v_col, 0), main(v_col), edge(v_col, 1)]
        args += [k_arr] * 3 + [v_arr] * 3
    specs += [ctx(kx_col), ctx(vx_col)]
    args += [kx_arr, vx_arr]
    return pl.pallas_call(
        functools.partial(_swa_kernel, windowed, nsub),
        grid=(b, SWA_KV_HEADS, s // tile),
        in_specs=specs,
        out_specs=qspec,
        out_shape=jax.ShapeDtypeStruct((b, s, SWA_HEADS * HEAD_DIM), BF16),
        compiler_params=_params("parallel", "parallel", "parallel"),
        name="swa_attention",
    )(*args)


def _diff_kernel(kc, lam_init, lam_ref, nrm_ref, q_ref, k_ref, v_ref, o_ref, sa_ref, sb_ref):
    tq = q_ref.shape[1]
    nk = k_ref.shape[1] // kc
    q = q_ref[0]
    lane = lax.broadcasted_iota(jnp.int32, q.shape, 1)
    zero = jnp.zeros_like(q)
    qmaps = (jnp.where(lane < DIFF_QK_DIM, q, zero), jnp.where(lane >= DIFF_QK_DIM, q, zero))
    ones_col = (lax.broadcasted_iota(jnp.int32, (kc, HEAD_DIM), 1) == 0).astype(BF16)

    def logits(j, s_ref):
        kblk = k_ref[0, pl.ds(pl.multiple_of(j * kc, kc), kc), :]
        for mi in range(2):
            s_ref[mi] = _dot_nt(qmaps[mi], kblk).astype(BF16)

    def softmax_pv(j, s_ref, carry):
        row0 = pl.multiple_of(j * kc, kc)
        out = []
        for mi in range(2):
            m, acc = carry[2 * mi:2 * mi + 2]
            m_new = jnp.maximum(m, jnp.max(s_ref[mi], axis=-1, keepdims=True).astype(F32))
            acc = jnp.exp(m - m_new) * acc
            e = jnp.exp(s_ref[mi] - m_new.astype(BF16))
            v1 = jnp.concatenate([v_ref[0, pl.ds(row0, kc), :], ones_col], axis=1)
            acc = acc + _dot(e, v1)
            out += [m_new, acc]
        return tuple(out)

    logits(0, sa_ref)

    def body(t, carry):
        logits(2 * t + 1, sb_ref)
        carry = softmax_pv(2 * t, sa_ref, carry)
        logits(jnp.minimum(2 * t + 2, nk - 1), sa_ref)
        return softmax_pv(2 * t + 1, sb_ref, carry)

    neg = jnp.full((tq, 1), -jnp.inf, F32)
    zacc = jnp.zeros((tq, 2 * HEAD_DIM), F32)
    carry = lax.fori_loop(0, nk // 2, body, (neg, zacc, neg, zacc), unroll=min(3, max(nk // 2, 1)))
    if nk % 2:
        carry = softmax_pv(nk - 1, sa_ref, carry)
    _, a0, _, a1 = carry
    lam = lam_ref[...]
    lam_full = (jnp.exp(jnp.sum(lam[0:1] * lam[1:2], axis=-1, keepdims=True))
                - jnp.exp(jnp.sum(lam[2:3] * lam[3:4], axis=-1, keepdims=True)) + lam_init)
    o = (a0[:, :HEAD_DIM] / a0[:, HEAD_DIM:HEAD_DIM + 1]
         - lam_full * (a1[:, :HEAD_DIM] / a1[:, HEAD_DIM:HEAD_DIM + 1]))
    o_ref[0] = (_rms(o, nrm_ref[...]) * (1.0 - lam_init)).astype(BF16)


def diff_attention(q, k, v, lam, norm_w, lam_init, tq=512):
    (q_arr, q_col), (k_arr, k_col), (v_arr, v_col) = q, k, v
    b, s, _ = q_arr.shape
    sk = k_arr.shape[1]
    tq = min(tq, s)
    kc = 1280 if sk % 1280 == 0 else 256
    assert s % tq == 0 and sk % kc == 0, (s, sk, tq, kc)
    return pl.pallas_call(
        functools.partial(_diff_kernel, kc, lam_init),
        grid=(b, DIFF_HEADS, s // tq),
        in_specs=[pl.BlockSpec((4, DIFF_QK_DIM), lambda bb, h, i: (0, 0)),
                  pl.BlockSpec((1, HEAD_DIM), lambda bb, h, i: (0, 0)),
                  pl.BlockSpec((1, tq, HEAD_DIM), lambda bb, h, i: (bb, i, q_col + h)),
                  pl.BlockSpec((1, sk, HEAD_DIM), lambda bb, h, i: (bb, 0, k_col + h)),
                  pl.BlockSpec((1, sk, HEAD_DIM), lambda bb, h, i: (bb, 0, v_col + h))],
        out_specs=pl.BlockSpec((1, tq, HEAD_DIM), lambda bb, h, i: (bb, i, h)),
        out_shape=jax.ShapeDtypeStruct((b, s, DIFF_HEADS * HEAD_DIM), BF16),
        scratch_shapes=[pltpu.VMEM((2, tq, kc), BF16), pltpu.VMEM((2, tq, kc), BF16)],
        compiler_params=_params("parallel", "parallel", "parallel"),
        name="diff_attention",
    )(lam, norm_w.reshape(1, HEAD_DIM), q_arr, k_arr, v_arr)


_O_QKV, _O_Z, _O_BETA, _O_ALPHA, _O_SQ, _O_SK, _O_SV, _O_DQ, _O_DK, _O_DV, _O_END = (
    0, 2304, 3072, 3084, 3096, 3864, 4120, 4376, 4888, 5400, 5912)


def _split_w_in(w):
    gates = jnp.zeros((w.shape[0], LANES), w.dtype).at[:, :_O_SQ - _O_BETA].set(w[:, _O_BETA:_O_SQ])
    w_gdn = jnp.concatenate([w[:, _O_QKV:_O_BETA], gates], axis=1)
    dq = w[:, _O_DQ:_O_DK] * (DIFF_QK_DIM ** -0.5)
    w_qk = jnp.concatenate([w[:, _O_SQ:_O_SV], dq, w[:, _O_DK:_O_DV]], axis=1)
    w_v = jnp.concatenate([w[:, _O_SV:_O_DQ], w[:, _O_DV:_O_END]], axis=1)
    return w_gdn.astype(BF16), w_qk.astype(BF16), w_v.astype(BF16)


def _project(h, w_gdn, w_qk, w_v, qk_dtype):
    b, s, d = h.shape
    h2 = h.reshape(b * s, d)
    p_gdn = matmul(h2, w_gdn, F32, tn=640, name="proj_gdn").reshape(b, s, -1)
    p_qk = matmul(h2, w_qk, qk_dtype, name="proj_qk").reshape(b, s, -1)
    p_v = matmul(h2, w_v, BF16, tn=768, name="proj_v").reshape(b, s, -1)
    return p_gdn, p_qk, p_v


def _gdn_consts(a_log, dt_bias):
    cst = jnp.zeros((8, LANES), F32)
    cst = cst.at[0, 2 * GDN_HEADS:4 * GDN_HEADS].set(-jnp.exp(a_log.astype(F32)).reshape(-1))
    cst = cst.at[1, 2 * GDN_HEADS:4 * GDN_HEADS].set(dt_bias.astype(F32).reshape(-1))
    return cst


def _layer(x, ctx, h_lat, h_ctx, m_lat, m_ctx, p, layer, with_ctx, rope_tabs, next_norm):
    b, s, d = x.shape
    n_ctx = ctx.shape[1]
    w_gdn, w_qk, w_v = _split_w_in(p["w_in"])
    w_out = p["w_out"].astype(BF16)
    lg, lqk, lv = _project(h_lat, w_gdn, w_qk, w_v, F32)
    cg, cqk, cv = _project(h_ctx, w_gdn, w_qk, w_v, BF16)

    cst = _gdn_consts(p["gdn_a_log"], p["gdn_dt_bias"])
    f_lat = gdn_features(lg, p["gdn_conv"])
    f_ctx = gdn_features(cg, p["gdn_conv"])
    s0 = jnp.zeros((b, GDN_HEADS, HEAD_DIM, HEAD_DIM), F32)
    o_lat, o_ctx = [], []
    for reverse in (False, True):
        oc, s_ctx = gdn_scan(f_ctx, cg, cst, s0, reverse)
        ol, _ = gdn_scan(f_lat, lg, cst, s_ctx, reverse)
        o_lat.append(ol)
        o_ctx.append(oc)
    a_lat = gdn_output(o_lat[0], o_lat[1], lg, p["gdn_norm"])

    lqk = rope_qk(lqk, *rope_tabs)
    sink = p["swa_sink"].astype(F32)
    sq_c, sk_c, dq_c, dk_c = 0, SWA_HEADS, SWA_HEADS + SWA_KV_HEADS, SWA_HEADS + SWA_KV_HEADS + DIFF_HEADS
    sv_c, dv_c = 0, SWA_KV_HEADS
    b_lat = swa_attention(lqk, (lqk, sk_c), (lv, sv_c), (cqk, sk_c), (cv, sv_c), sink, True)
    lam_init = 0.8 - 0.6 * math.exp(-0.3 * layer)
    lam = p["diff_lambda"].astype(F32)
    dk_all = jnp.concatenate([lqk[:, :, dk_c * LANES:], cqk[:, :, dk_c * LANES:]], axis=1)
    dv_all = jnp.concatenate([lv[:, :, dv_c * LANES:], cv[:, :, dv_c * LANES:]], axis=1)
    c_lat = diff_attention((lqk, dq_c), (dk_all, 0), (dv_all, 0), lam, p["diff_norm"], lam_init)
    mix = jnp.concatenate([a_lat, b_lat, c_lat], axis=-1).reshape(b * s, d)
    y_lat = matmul(mix, w_out, F32, name="proj_out").reshape(b, s, d)

    x, hf_lat = norm_step(x, resid=(y_lat, m_lat[2], p["norm_post_mix"]),
                          norm=(p["norm_pre_ffn"], m_lat[3], m_lat[4]))
    if with_ctx:
        a_ctx = gdn_output(o_ctx[0], o_ctx[1], cg, p["gdn_norm"])
        b_ctx = swa_attention(cqk, None, None, (cqk, sk_c), (cv, sv_c), sink, False)
        c_ctx = diff_attention((cqk, dq_c), (cqk, dk_c), (cv, dv_c), lam, p["diff_norm"], lam_init)
        mix_c = jnp.concatenate([a_ctx, b_ctx, c_ctx], axis=-1).reshape(b * n_ctx, d)
        y_ctx = matmul(mix_c, w_out, F32, name="proj_out").reshape(b, n_ctx, d)
        ctx, hf_ctx = norm_step(ctx, resid=(y_ctx, m_ctx[2], p["norm_post_mix"]),
                                norm=(p["norm_pre_ffn"], m_ctx[3], m_ctx[4]))

    w_gate = p["ffn_w_gate"].astype(BF16)
    w_up = p["ffn_w_up"].astype(BF16)
    w_down = p["ffn_w_down"].astype(BF16)

    def ffn(h, n_tok):
        h2 = h.reshape(b * n_tok, d)
        act = ffn_up(h2, w_gate, w_up, p["ffn_conv"], n_tok)
        return matmul(act, w_down, F32, tm=1024, name="ffn_down").reshape(b, n_tok, d)

    f_lat = ffn(hf_lat, s)
    x, h_lat = norm_step(x, resid=(f_lat, m_lat[5], p["norm_post_ffn"]),
                         norm=None if next_norm is None else next_norm[0])
    h_ctx = None
    if with_ctx:
        f_ctx = ffn(hf_ctx, n_ctx)
        ctx, h_ctx = norm_step(ctx, resid=(f_ctx, m_ctx[5], p["norm_post_ffn"]),
                               norm=None if next_norm is None else next_norm[1])
    return x, ctx, h_lat, h_ctx


def kernel(x, c, ctx, c_ctx, w_mod, b_mod, norm_pre_mix, norm_post_mix, norm_pre_ffn, norm_post_ffn,
           w_in, gdn_conv, gdn_a_log, gdn_dt_bias, gdn_norm, swa_sink, diff_lambda, diff_norm, w_out,
           ffn_w_gate, ffn_w_up, ffn_conv, ffn_w_down):
    b, s, d = x.shape
    depth = w_mod.shape[0]
    rope_tabs = (_rope_tables(s, HEAD_DIM), _rope_tables(s, DIFF_QK_DIM))

    cc = jnp.zeros((8, d), F32).at[:b].set(c).at[b].set(c_ctx)
    mods = []
    for layer in range(depth):
        m = modulation(cc, w_mod[layer], b_mod[layer])
        m_lat = [m[:b, None, j * d:(j + 1) * d] for j in range(6)]
        m_ctx = [jnp.broadcast_to(m[b:b + 1, None, j * d:(j + 1) * d], (b, 1, d)) for j in range(6)]
        mods.append((m_lat, m_ctx))

    def pre_mix(layer):
        m_lat, m_ctx = mods[layer]
        return ((norm_pre_mix[layer], m_lat[0], m_lat[1]), (norm_pre_mix[layer], m_ctx[0], m_ctx[1]))

    n0 = pre_mix(0)
    _, h_lat = norm_step(x, norm=n0[0])
    _, h_ctx = norm_step(ctx, norm=n0[1])
    for layer in range(depth):
        with_ctx = layer < depth - 1
        p = dict(w_in=w_in[layer], gdn_conv=gdn_conv[layer], gdn_a_log=gdn_a_log[layer],
                 gdn_dt_bias=gdn_dt_bias[layer], gdn_norm=gdn_norm[layer], swa_sink=swa_sink[layer],
                 diff_lambda=diff_lambda[layer], diff_norm=diff_norm[layer], w_out=w_out[layer],
                 norm_post_mix=norm_post_mix[layer], norm_pre_ffn=norm_pre_ffn[layer],
                 norm_post_ffn=norm_post_ffn[layer], ffn_w_gate=ffn_w_gate[layer],
                 ffn_w_up=ffn_w_up[layer], ffn_conv=ffn_conv[layer], ffn_w_down=ffn_w_down[layer])
        m_lat, m_ctx = mods[layer]
        next_norm = pre_mix(layer + 1) if layer + 1 < depth else None
        x, ctx, h_lat, h_ctx = _layer(x, ctx, h_lat, h_ctx, m_lat, m_ctx, p, layer, with_ctx,
                                      rope_tabs, next_norm)
    return x
```
